```python
import jax, jax.numpy as jnp
from jax import lax
import numpy as np

D_MODEL = 2048
BATCH = 4
SEQ = 2048
DEPTH = 4
DEC_BATCH = 128
DEC_SEQ = 4
PAST_LEN = 16384
PAGE_SIZE = 128

CHUNK = 128
D_A = D_MODEL // 2
D_B = D_MODEL // 2
D_C = D_MODEL // 2
G_A = D_A // CHUNK
K_CONV = 31
POOL_WINDOWS = (2, 4, 8, 16)
N_POOL_GROUPS = len(POOL_WINDOWS)
D_CG = D_C // N_POOL_GROUPS
D_CG_OUT = D_MODEL // N_POOL_GROUPS
POOL_BUF = max(POOL_WINDOWS) - 1
K_FFN = 3
D_FF = (((8 * D_MODEL) // 3 + 127) // 128) * 128
N_BRANCH = 3
N_IN = 2 * D_A + 2 * D_B + D_C + N_BRANCH * D_MODEL
EPS = 1e-6

kernel_name = 'hybrid_gmlp_conformer_pool_decoder_step'


def rmsnorm(x, g):
    x32 = x.astype(jnp.float32)
    y = x32 * lax.rsqrt(jnp.mean(x32 * x32, axis=-1, keepdims=True) + EPS)
    return (y * g.astype(jnp.float32)).astype(x.dtype)


def layernorm(x, g, b):
    x32 = x.astype(jnp.float32)
    mu = jnp.mean(x32, axis=-1, keepdims=True)
    var = jnp.mean(jnp.square(x32 - mu), axis=-1, keepdims=True)
    y = (x32 - mu) * lax.rsqrt(var + EPS)
    return (y * g.astype(jnp.float32) + b.astype(jnp.float32)).astype(x.dtype)


def causal_dwconv(x, buf, w, b):
    k = w.shape[0]
    xcat = jnp.concatenate([buf.astype(x.dtype), x], axis=1)
    y = lax.conv_general_dilated(xcat, w.astype(x.dtype)[:, None, :], window_strides=(1,),
                                 padding='VALID', dimension_numbers=('NWC', 'WIO', 'NWC'),
                                 feature_group_count=x.shape[-1])
    return y + b.astype(x.dtype), xcat[:, xcat.shape[1] - (k - 1):]


def chunk_spatial_gate(u, v, w_s, b_s):
    n, t, c = v.shape
    n_chunks = -(-t // CHUNK)
    pad = n_chunks * CHUNK - t
    vp = jnp.pad(v, ((0, 0), (0, pad), (0, 0))).reshape(n, n_chunks, CHUNK, G_A, c // G_A)
    mask = jnp.tril(jnp.ones((CHUNK, CHUNK), dtype=bool))
    w = jnp.where(mask[None], w_s, jnp.zeros_like(w_s))
    mixed = jnp.einsum('gts,bnsgc->bntgc', w, vp) + b_s.T[None, None, :, :, None]
    mixed = mixed.reshape(n, n_chunks * CHUNK, c)[:, :t]
    return u * mixed


def multiscale_pool(p, buf, start):
    n, t, c = p.shape
    L = buf.shape[1]
    xcat = jnp.concatenate([buf.astype(p.dtype), p], axis=1)
    cs = jnp.concatenate([jnp.zeros((n, 1, c), jnp.float32),
                          jnp.cumsum(xcat.astype(jnp.float32), axis=1)], axis=1)
    pos = start + jnp.arange(t, dtype=jnp.int32)
    outs = []
    for gi, w in enumerate(POOL_WINDOWS):
        lo, hi = gi * D_CG, (gi + 1) * D_CG
        s = cs[:, L + 1:L + 1 + t, lo:hi] - cs[:, L + 1 - w:L + 1 - w + t, lo:hi]
        cnt = jnp.minimum(pos + 1, w).astype(jnp.float32)[None, :, None]
        outs.append(s / cnt)
    pooled = jnp.concatenate(outs, axis=-1) - p.astype(jnp.float32)
    return pooled.astype(p.dtype), xcat[:, xcat.shape[1] - L:]


def hybrid_layer(x, c, conv_buf, pool_buf, ffn_buf, start, ada_w, ada_b, g_pre_mix, g_post_mix,
                 g_pre_ffn, g_post_ffn, w_in, b_gate, ln_v_g, ln_v_b, w_spatial, b_spatial, w_a_out,
                 w_dwconv, b_dwconv, ln_conv_g, ln_conv_b, w_b_out, w_pool_grp, pool_scale, w_o,
                 w_up, w_ffn_conv, b_ffn_conv, w_down):
    n, t, _ = x.shape
    mod = (jax.nn.silu(c) @ ada_w + ada_b)[:, None, :]
    shift_m, scale_m, gate_m, shift_f, scale_f, gate_f = jnp.split(mod, 6, axis=-1)

    h = rmsnorm(x, g_pre_mix) * (1 + scale_m) + shift_m
    proj = h @ w_in
    u, v, glu, pin, gates_pre = jnp.split(
        proj, [D_A, 2 * D_A, 2 * D_A + 2 * D_B, 2 * D_A + 2 * D_B + D_C], axis=-1)

    v_n = layernorm(jax.nn.gelu(v), ln_v_g, ln_v_b)
    y_a = chunk_spatial_gate(jax.nn.gelu(u), v_n, w_spatial, b_spatial) @ w_a_out

    ga, gb = jnp.split(glu, 2, axis=-1)
    xb = ga * jax.nn.sigmoid(gb)
    yb, new_conv = causal_dwconv(xb, conv_buf, w_dwconv, b_dwconv)
    y_b = jax.nn.silu(layernorm(yb, ln_conv_g, ln_conv_b)) @ w_b_out

    pooled, new_pool = multiscale_pool(pin, pool_buf, start)
    y_c = jnp.einsum('btgc,gcd->btgd', pooled.reshape(n, t, N_POOL_GROUPS, D_CG), w_pool_grp)
    y_c = y_c.reshape(n, t, D_MODEL) * pool_scale

    g_a, g_b, g_c = jnp.split(jax.nn.sigmoid(gates_pre + b_gate), N_BRANCH, axis=-1)
    mix = (g_a * y_a + g_b * y_b + g_c * y_c) @ w_o
    x = x + gate_m * rmsnorm(mix, g_post_mix)

    h = rmsnorm(x, g_pre_ffn) * (1 + scale_f) + shift_f
    gp, val = jnp.split(h @ w_up, 2, axis=-1)
    gc, new_ffn = causal_dwconv(gp, ffn_buf, w_ffn_conv, b_ffn_conv)
    y = (jax.nn.gelu(gc) * val) @ w_down
    x = x + gate_f * rmsnorm(y, g_post_ffn)
    return x, new_conv, new_pool, new_ffn, v_n


def setup_inputs(seed: int = 0) -> dict:
    key = jax.random.key(seed)
    ks = jax.random.split(key, 40)

    def nrm(k, shape, scale):
        return jax.random.normal(k, shape, jnp.float32) * scale

    def gain(k, shape):
        return 1.0 + nrm(k, shape, 0.05)

    return {
        'x_prompt': nrm(ks[0], (BATCH, SEQ, D_MODEL), 1.0),
        'x_sample': nrm(ks[1], (DEC_BATCH, DEC_SEQ, D_MODEL), 1.0),
        'c_prompt': nrm(ks[2], (BATCH, D_MODEL), 1.0),
        'c_sample': nrm(ks[3], (DEC_BATCH, D_MODEL), 1.0),
        'state_conv': nrm(ks[4], (DEPTH, DEC_BATCH, K_CONV - 1, D_B), 0.5),
        'state_pool': nrm(ks[5], (DEPTH, DEC_BATCH, POOL_BUF, D_C), 1.0),
        'state_ffn_conv': nrm(ks[6], (DEPTH, DEC_BATCH, K_FFN - 1, D_FF), 1.0),
        'ada_w': nrm(ks[7], (DEPTH, D_MODEL, 6 * D_MODEL), 0.5 * D_MODEL ** -0.5),
        'ada_b': nrm(ks[8], (DEPTH, 6 * D_MODEL), 0.02),
        'g_pre_mix': gain(ks[9], (DEPTH, D_MODEL)),
        'g_post_mix': gain(ks[10], (DEPTH, D_MODEL)),
        'g_pre_ffn': gain(ks[11], (DEPTH, D_MODEL)),
        'g_post_ffn': gain(ks[12], (DEPTH, D_MODEL)),
        'w_in': nrm(ks[13], (DEPTH, D_MODEL, N_IN), D_MODEL ** -0.5),
        'b_gate': nrm(ks[14], (DEPTH, N_BRANCH * D_MODEL), 0.02),
        'ln_v_g': gain(ks[15], (DEPTH, D_A)),
        'ln_v_b': nrm(ks[16], (DEPTH, D_A), 0.02),
        'w_spatial': nrm(ks[17], (DEPTH, G_A, CHUNK, CHUNK), CHUNK ** -0.5),
        'b_spatial': 1.0 + nrm(ks[18], (DEPTH, G_A, CHUNK), 0.1),
        'w_a_out': nrm(ks[19], (DEPTH, D_A, D_MODEL), D_A ** -0.5),
        'w_dwconv': nrm(ks[20], (DEPTH, K_CONV, D_B), K_CONV ** -0.5),
        'b_dwconv': nrm(ks[21], (DEPTH, D_B), 0.02),
        'ln_conv_g': gain(ks[22], (DEPTH, D_B)),
        'ln_conv_b': nrm(ks[23], (DEPTH, D_B), 0.02),
        'w_b_out': nrm(ks[24], (DEPTH, D_B, D_MODEL), D_B ** -0.5),
        'w_pool_grp': nrm(ks[25], (DEPTH, N_POOL_GROUPS, D_CG, D_CG_OUT), D_CG ** -0.5),
        'pool_scale': gain(ks[26], (DEPTH, D_MODEL)),
        'w_o': nrm(ks[27], (DEPTH, D_MODEL, D_MODEL), D_MODEL ** -0.5),
        'w_up': nrm(ks[28], (DEPTH, D_MODEL, 2 * D_FF), D_MODEL ** -0.5),
        'w_ffn_conv': nrm(ks[29], (DEPTH, K_FFN, D_FF), K_FFN ** -0.5),
        'b_ffn_conv': nrm(ks[30], (DEPTH, D_FF), 0.02),
        'w_down': nrm(ks[31], (DEPTH, D_FF, D_MODEL), D_FF ** -0.5),
    }


def reference(x_prompt, x_sample, c_prompt, c_sample, state_conv, state_pool, state_ffn_conv,
              ada_w, ada_b, g_pre_mix, g_post_mix, g_pre_ffn, g_post_ffn, w_in, b_gate,
              ln_v_g, ln_v_b, w_spatial, b_spatial, w_a_out, w_dwconv, b_dwconv, ln_conv_g,
              ln_conv_b, w_b_out, w_pool_grp, pool_scale, w_o, w_up, w_ffn_conv, b_ffn_conv,
              w_down):
    x_p, x_s = x_prompt, x_sample
    conv_p, conv_s, pool_p, pool_s, ffn_p, ffn_s, v_s = [], [], [], [], [], [], []
    for l in range(DEPTH):
        lw = dict(ada_w=ada_w[l], ada_b=ada_b[l], g_pre_mix=g_pre_mix[l], g_post_mix=g_post_mix[l],
                  g_pre_ffn=g_pre_ffn[l], g_post_ffn=g_post_ffn[l], w_in=w_in[l], b_gate=b_gate[l],
                  ln_v_g=ln_v_g[l], ln_v_b=ln_v_b[l], w_spatial=w_spatial[l], b_spatial=b_spatial[l],
                  w_a_out=w_a_out[l], w_dwconv=w_dwconv[l], b_dwconv=b_dwconv[l],
                  ln_conv_g=ln_conv_g[l], ln_conv_b=ln_conv_b[l], w_b_out=w_b_out[l],
                  w_pool_grp=w_pool_grp[l], pool_scale=pool_scale[l], w_o=w_o[l], w_up=w_up[l],
                  w_ffn_conv=w_ffn_conv[l], b_ffn_conv=b_ffn_conv[l], w_down=w_down[l])
        zc = jnp.zeros((x_p.shape[0], K_CONV - 1, D_B), x_p.dtype)
        zp = jnp.zeros((x_p.shape[0], POOL_BUF, D_C), x_p.dtype)
        zf = jnp.zeros((x_p.shape[0], K_FFN - 1, D_FF), x_p.dtype)
        x_p, cp, pp, fp, _ = hybrid_layer(x_p, c_prompt, zc, zp, zf, 0, **lw)
        x_s, cs_, ps_, fs_, vs_ = hybrid_layer(x_s, c_sample, state_conv[l], state_pool[l],
                                               state_ffn_conv[l], PAST_LEN, **lw)
        conv_p.append(cp); pool_p.append(pp); ffn_p.append(fp)
        conv_s.append(cs_); pool_s.append(ps_); ffn_s.append(fs_); v_s.append(vs_)
    return (x_p, x_s, jnp.stack(conv_p), jnp.stack(conv_s), jnp.stack(pool_p), jnp.stack(pool_s),
            jnp.stack(ffn_p), jnp.stack(ffn_s), jnp.stack(v_s))
```

```python
import functools

import jax
import jax.numpy as jnp
from jax import lax
from jax.experimental import pallas as pl
from jax.experimental.pallas import tpu as pltpu

F32 = jnp.float32
BF16 = jnp.bfloat16

D_MODEL = 2048
DEPTH = 4
PAST_LEN = 16384
CHUNK = 128
D_A = D_MODEL // 2
D_B = D_MODEL // 2
D_C = D_MODEL // 2
G_A = D_A // CHUNK
K_CONV = 31
POOL_WINDOWS = (2, 4, 8, 16)
D_CG = D_C // len(POOL_WINDOWS)
D_CG_OUT = D_MODEL // len(POOL_WINDOWS)
POOL_BUF = max(POOL_WINDOWS) - 1
K_FFN = 3
D_FF = 5504
N_IN = 2 * D_A + 2 * D_B + D_C + 3 * D_MODEL
EPS = 1e-6

COL_U, COL_V, COL_GA, COL_GB, COL_PIN, COL_GATES = 0, 1, 2, 3, 4, 5

CONV_HALO = 32
POOL_HALO = 16
CONV_ROWS = 16

VMEM_LIMIT = 56 * 1024 * 1024


def _params(sem):
    return pltpu.CompilerParams(dimension_semantics=sem, vmem_limit_bytes=VMEM_LIMIT)


def _gelu(x):
    return jax.nn.gelu(x, approximate=True)


def _rmsnorm(x, g):
    return x * lax.rsqrt(jnp.mean(x * x, axis=-1, keepdims=True) + EPS) * g


def _layernorm(x, g, b):
    mu = jnp.mean(x, axis=-1, keepdims=True)
    xc = x - mu
    var = jnp.mean(xc * xc, axis=-1, keepdims=True)
    return xc * lax.rsqrt(var + EPS) * g + b


def _rows(m, tm):
    rm, c = m.shape
    if rm == 1 or rm == tm:
        return m
    return jnp.broadcast_to(m[None], (tm // rm, rm, c)).reshape(tm, c)


def _mods_kernel(c_ref, w_ref, b_ref, o_ref):
    c = c_ref[...]
    a = (c * jax.nn.sigmoid(c)).astype(BF16)
    o_ref[...] = jnp.dot(a, w_ref[...].astype(BF16), preferred_element_type=F32) + b_ref[...]


def _mods(c_all, ada_w, ada_b, tn=1024):
    rows = c_all.shape[0]
    n_out = ada_w.shape[-1]
    return pl.pallas_call(
        _mods_kernel,
        grid=(DEPTH, n_out // tn),
        in_specs=[
            pl.BlockSpec((rows, D_MODEL), lambda l, j: (0, 0)),
            pl.BlockSpec((None, D_MODEL, tn), lambda l, j: (l, 0, j)),
            pl.BlockSpec((None, 1, tn), lambda l, j: (l, 0, j)),
        ],
        out_specs=pl.BlockSpec((None, rows, tn), lambda l, j: (l, 0, j)),
        out_shape=jax.ShapeDtypeStruct((DEPTH, rows, n_out), F32),
        compiler_params=_params(("arbitrary", "arbitrary")),
        name="mods",
    )(c_all, ada_w, ada_b.reshape(DEPTH, 1, n_out))


class _Group:
    def __init__(self, n_seq, t_len, time_major):
        self.n_seq = n_seq
        self.t_len = t_len
        self.time_major = time_major
        self.rows = n_seq * t_len

    def mod_spec(self, tm, chunk):
        if self.time_major:
            return pl.BlockSpec((self.n_seq, D_MODEL), lambda i: (0, chunk))
        per_seq = self.t_len // tm
        return pl.BlockSpec((None, 1, D_MODEL), lambda i: (i // per_seq, 0, chunk))

    def mod_array(self, mod):
        return mod if self.time_major else mod.reshape(self.n_seq, 1, mod.shape[-1])


def _hist_row(ref, j, hist, nb, width):
    nc = width // 128
    return jnp.concatenate(
        [ref[pl.ds(j * nc + c, nb, stride=hist * nc), :] for c in range(nc)], axis=-1)


def _vec_spec(width):
    return pl.BlockSpec((1, width), lambda *_: (0, 0))


def _prenorm_kernel(x_ref, g_ref, sc_ref, sh_ref, h_ref):
    x = x_ref[...]
    tm = x.shape[0]
    y = _rmsnorm(x, g_ref[...])
    h_ref[...] = (y * (1.0 + _rows(sc_ref[...], tm)) + _rows(sh_ref[...], tm)).astype(BF16)


def _prenorm(grp, x, g, mod, c_scale, c_shift, tm):
    return pl.pallas_call(
        _prenorm_kernel,
        grid=(grp.rows // tm,),
        in_specs=[
            pl.BlockSpec((tm, D_MODEL), lambda i: (i, 0)),
            _vec_spec(D_MODEL),
            grp.mod_spec(tm, c_scale),
            grp.mod_spec(tm, c_shift),
        ],
        out_specs=pl.BlockSpec((tm, D_MODEL), lambda i: (i, 0)),
        out_shape=jax.ShapeDtypeStruct((grp.rows, D_MODEL), BF16),
        compiler_params=_params(("arbitrary",)),
        name="prenorm",
    )(x, g.reshape(1, D_MODEL), grp.mod_array(mod), grp.mod_array(mod))


def _mm_kernel(a_ref, b_ref, o_ref):
    o_ref[...] = jnp.dot(a_ref[...], b_ref[...], preferred_element_type=F32).astype(o_ref.dtype)


def _mm(a, b, tm, tn, name):
    m, k = a.shape
    n = b.shape[1]
    return pl.pallas_call(
        _mm_kernel,
        grid=(pl.cdiv(n, tn), m // tm),
        in_specs=[
            pl.BlockSpec((tm, k), lambda j, i: (i, 0)),
            pl.BlockSpec((k, tn), lambda j, i: (0, j)),
        ],
        out_specs=pl.BlockSpec((tm, tn), lambda j, i: (i, j)),
        out_shape=jax.ShapeDtypeStruct((m, n), BF16),
        compiler_params=_params(("arbitrary", "arbitrary")),
        name=name,
    )(a, b)


def _seq_p_kernel(u_ref, v_ref, ga_ref, gb_ref, pin_ref, lnvg_ref, lnvb_ref, ws_ref, bmix_ref,
                  wconv_ref, bconv_ref, lncg_ref, lncb_ref,
                  acts_ref, nconv_ref, xcat, pcat, *, tm, n_tiles, start):
    i = pl.program_id(1)

    vn = _layernorm(_gelu(v_ref[...].astype(F32)), lnvg_ref[...], lnvb_ref[...]).astype(BF16)
    gu = _gelu(u_ref[...].astype(F32))
    row = lax.broadcasted_iota(jnp.int32, (CHUNK, CHUNK), 0)
    col = lax.broadcasted_iota(jnp.int32, (CHUNK, CHUNK), 1)
    for g in range(G_A):
        lanes = slice(g * CHUNK, (g + 1) * CHUNK)
        wg = jnp.where(row >= col, ws_ref[g], 0.0).astype(BF16)
        for c in range(tm // CHUNK):
            rws = slice(c * CHUNK, (c + 1) * CHUNK)
            mixed = jnp.dot(wg, vn[rws, lanes], preferred_element_type=F32) + bmix_ref[:, lanes]
            acts_ref[rws, lanes] = (gu[rws, lanes] * mixed).astype(BF16)

    @pl.when(i == 0)
    def _():
        xcat[0:CONV_HALO, :] = jnp.zeros((CONV_HALO, D_B), F32)
        pcat[0:POOL_HALO, :] = jnp.zeros((POOL_HALO, D_C), F32)

    @pl.when(i > 0)
    def _():
        xcat[0:CONV_HALO, :] = xcat[tm:tm + CONV_HALO, :]
        pcat[0:POOL_HALO, :] = pcat[tm:tm + POOL_HALO, :]

    xcat[CONV_HALO:CONV_HALO + tm, :] = (
        ga_ref[...].astype(F32) * jax.nn.sigmoid(gb_ref[...].astype(F32)))
    off = CONV_HALO - (K_CONV - 1)
    for r in range(tm // CONV_ROWS):
        base = r * CONV_ROWS
        acc = jnp.broadcast_to(bconv_ref[...], (CONV_ROWS, D_B))
        for k in range(K_CONV):
            acc = acc + wconv_ref[k:k + 1, :] * xcat[base + off + k:base + off + k + CONV_ROWS, :]
        zb = _layernorm(acc, lncg_ref[...], lncb_ref[...])
        acts_ref[base:base + CONV_ROWS, D_A:D_A + D_B] = (zb * jax.nn.sigmoid(zb)).astype(BF16)

    @pl.when(i == n_tiles - 1)
    def _():
        nconv_ref[...] = xcat[CONV_HALO + tm - (K_CONV - 1):CONV_HALO + tm, :]

    p = pin_ref[...].astype(F32)
    pcat[POOL_HALO:POOL_HALO + tm, :] = p
    pos = start + i * tm + lax.broadcasted_iota(jnp.int32, (tm, 1), 0)
    for gi, w in enumerate(POOL_WINDOWS):
        lanes = slice(gi * D_CG, (gi + 1) * D_CG)
        s = p[:, lanes]
        for j in range(1, w):
            s = s + pcat[POOL_HALO - j:POOL_HALO - j + tm, lanes]
        cnt = jnp.minimum(pos + 1, w).astype(F32)
        acts_ref[:, D_A + D_B + gi * D_CG:D_A + D_B + (gi + 1) * D_CG] = (
            s / cnt - p[:, lanes]).astype(BF16)


def _seq_p(grp, proj, lw, tm):
    n_tiles = grp.t_len // tm

    def col(c):
        return pl.BlockSpec((tm, 1024), lambda n, i: (n * n_tiles + i, c))

    full = lambda shape: pl.BlockSpec(shape, lambda n, i: (0,) * len(shape))
    return pl.pallas_call(
        functools.partial(_seq_p_kernel, tm=tm, n_tiles=n_tiles, start=0),
        grid=(grp.n_seq, n_tiles),
        in_specs=[col(COL_U), col(COL_V), col(COL_GA), col(COL_GB), col(COL_PIN),
                  full((1, D_A)), full((1, D_A)), full((G_A, CHUNK, CHUNK)), full((CHUNK, D_A)),
                  full((K_CONV, D_B)), full((1, D_B)), full((1, D_B)), full((1, D_B))],
        out_specs=[pl.BlockSpec((tm, 3 * 1024), lambda n, i: (n * n_tiles + i, 0)),
                   pl.BlockSpec((None, K_CONV - 1, D_B), lambda n, i: (n, 0, 0))],
        out_shape=[jax.ShapeDtypeStruct((grp.rows, 3 * 1024), BF16),
                   jax.ShapeDtypeStruct((grp.n_seq, K_CONV - 1, D_B), F32)],
        scratch_shapes=[pltpu.VMEM((CONV_HALO + tm, D_B), F32),
                        pltpu.VMEM((POOL_HALO + tm, D_C), F32)],
        compiler_params=_params(("arbitrary", "arbitrary")),
        name="seq_prompt",
    )(proj, proj, proj, proj, proj, lw["ln_v_g"], lw["ln_v_b"], lw["w_spatial"], lw["b_mix"],
      lw["w_dwconv"], lw["b_dwconv"], lw["ln_conv_g"], lw["ln_conv_b"])


def _seq_s_kernel(u_ref, v_ref, ga_ref, gb_ref, pin_ref, sconv_ref, spool_ref,
                  lnvg_ref, lnvb_ref, wv_ref, bv_ref, wconv_ref, bconv_ref, lncg_ref, lncb_ref,
                  acts_ref, vn_ref, xb_ref, *, t_len, nb, start):
    vn = []
    for t in range(t_len):
        vn_t = _layernorm(_gelu(v_ref[t].astype(F32)), lnvg_ref[...], lnvb_ref[...])
        vn_ref[t] = vn_t
        vn.append(vn_t)
        mixed = jnp.broadcast_to(bv_ref[t:t + 1, :], (nb, D_A))
        for s in range(t + 1):
            mixed = mixed + wv_ref[t * t_len + s:t * t_len + s + 1, :] * vn[s]
        acts_ref[t, :, 0:D_A] = (_gelu(u_ref[t].astype(F32)) * mixed).astype(BF16)

    hist = K_CONV - 1
    acc = [jnp.broadcast_to(bconv_ref[...], (nb, D_B)) for _ in range(t_len)]
    for j in range(hist + t_len):
        if j < hist:
            xj = _hist_row(sconv_ref, j, hist, nb, D_B)
        else:
            xj = ga_ref[j - hist].astype(F32) * jax.nn.sigmoid(gb_ref[j - hist].astype(F32))
            xb_ref[j - hist] = xj
        for t in range(t_len):
            k = j - t
            if 0 <= k < K_CONV:
                acc[t] = acc[t] + wconv_ref[k:k + 1, :] * xj
    for t in range(t_len):
        zb = _layernorm(acc[t], lncg_ref[...], lncb_ref[...])
        acts_ref[t, :, D_A:D_A + D_B] = (zb * jax.nn.sigmoid(zb)).astype(BF16)

    rows = [_hist_row(spool_ref, j, POOL_BUF, nb, D_C) for j in range(POOL_BUF)]
    rows += [pin_ref[t].astype(F32) for t in range(t_len)]
    for t in range(t_len):
        for gi, w in enumerate(POOL_WINDOWS):
            lanes = slice(gi * D_CG, (gi + 1) * D_CG)
            s = rows[POOL_BUF + t][:, lanes]
            for j in range(1, w):
                s = s + rows[POOL_BUF + t - j][:, lanes]
            cnt = float(min(start + t + 1, w))
            acts_ref[t, :, D_A + D_B + gi * D_CG:D_A + D_B + (gi + 1) * D_CG] = (
                s / cnt - rows[POOL_BUF + t][:, lanes]).astype(BF16)


def _seq_s(grp, proj, state_conv, state_pool, lw, nb=32):
    t_len, n_seq = grp.t_len, grp.n_seq
    proj3 = proj.reshape(t_len, n_seq, N_IN)

    def col(c):
        return pl.BlockSpec((t_len, nb, 1024), lambda b: (0, b, c))

    full = lambda shape: pl.BlockSpec(shape, lambda b: (0,) * len(shape))
    hist = K_CONV - 1
    return pl.pallas_call(
        functools.partial(_seq_s_kernel, t_len=t_len, nb=nb, start=PAST_LEN),
        grid=(n_seq // nb,),
        in_specs=[col(COL_U), col(COL_V), col(COL_GA), col(COL_GB), col(COL_PIN),
                  pl.BlockSpec((nb * hist * D_B // 128, 128), lambda b: (b, 0)),
                  pl.BlockSpec((nb * POOL_BUF * D_C // 128, 128), lambda b: (b, 0)),
                  full((1, D_A)), full((1, D_A)), full((t_len * t_len, D_A)), full((t_len, D_A)),
                  full((K_CONV, D_B)), full((1, D_B)), full((1, D_B)), full((1, D_B))],
        out_specs=[pl.BlockSpec((t_len, nb, 3 * 1024), lambda b: (0, b, 0)),
                   pl.BlockSpec((t_len, nb, D_A), lambda b: (0, b, 0)),
                   pl.BlockSpec((t_len, nb, D_B), lambda b: (0, b, 0))],
        out_shape=[jax.ShapeDtypeStruct((t_len, n_seq, 3 * 1024), BF16),
                   jax.ShapeDtypeStruct((t_len, n_seq, D_A), F32),
                   jax.ShapeDtypeStruct((t_len, n_seq, D_B), F32)],
        compiler_params=_params(("arbitrary",)),
        name="seq_sample",
    )(proj3, proj3, proj3, proj3, proj3,
      state_conv.reshape(-1, 128), state_pool.reshape(-1, 128),
      lw["ln_v_g"], lw["ln_v_b"], lw["w_spatial_vec"], lw["b_spatial_vec"],
      lw["w_dwconv"], lw["b_dwconv"], lw["ln_conv_g"], lw["ln_conv_b"])


def _mixmm_kernel(acts_ref, g0, g1, g2, g3, g4, g5, bgate_ref, wa_ref, wb_ref, wp_ref, ps_ref,
                  o_ref):
    gate_refs = (g0, g1, g2, g3, g4, g5)

    def gate(branch, half):
        idx = 2 * branch + half
        pre = gate_refs[idx][...].astype(F32) + bgate_ref[:, idx * 1024:(idx + 1) * 1024]
        return jax.nn.sigmoid(pre)

    ya = jnp.dot(acts_ref[:, 0:D_A], wa_ref[...], preferred_element_type=F32)
    yb = jnp.dot(acts_ref[:, D_A:D_A + D_B], wb_ref[...], preferred_element_type=F32)
    for half in range(2):
        cols = slice(half * 1024, (half + 1) * 1024)
        yc = jnp.concatenate(
            [jnp.dot(acts_ref[:, D_A + D_B + g * D_CG:D_A + D_B + (g + 1) * D_CG], wp_ref[g],
                     preferred_element_type=F32) for g in (2 * half, 2 * half + 1)],
            axis=-1) * ps_ref[:, cols]
        mix = gate(0, half) * ya[:, cols] + gate(1, half) * yb[:, cols] + gate(2, half) * yc
        o_ref[:, cols] = mix.astype(BF16)


def _mixmm(acts, proj, lw, tm):
    m = acts.shape[0]

    def gcol(c):
        return pl.BlockSpec((tm, 1024), lambda i: (i, COL_GATES + c))

    full = lambda shape: pl.BlockSpec(shape, lambda i: (0,) * len(shape))
    return pl.pallas_call(
        _mixmm_kernel,
        grid=(m // tm,),
        in_specs=[pl.BlockSpec((tm, 3 * 1024), lambda i: (i, 0))] + [gcol(c) for c in range(6)] + [
            full((1, 3 * D_MODEL)), full((D_A, D_MODEL)), full((D_B, D_MODEL)),
            full((len(POOL_WINDOWS), D_CG, D_CG_OUT)), full((1, D_MODEL))],
        out_specs=pl.BlockSpec((tm, D_MODEL), lambda i: (i, 0)),
        out_shape=jax.ShapeDtypeStruct((m, D_MODEL), BF16),
        compiler_params=_params(("arbitrary",)),
        name="mixmm",
    )(acts, proj, proj, proj, proj, proj, proj, lw["b_gate"], lw["w_a_out"], lw["w_b_out"],
      lw["w_pool_grp"], lw["pool_scale"])


def _g1_kernel(a_ref, w_ref, x_ref, gate_ref, gpost_ref, gnext_ref, sc_ref, sh_ref,
               xo_ref, ho_ref):
    tm = x_ref.shape[0]
    y = jnp.dot(a_ref[...], w_ref[...], preferred_element_type=F32)
    x = x_ref[...] + _rows(gate_ref[...], tm) * _rmsnorm(y, gpost_ref[...])
    xo_ref[...] = x
    h = _rmsnorm(x, gnext_ref[...]) * (1.0 + _rows(sc_ref[...], tm)) + _rows(sh_ref[...], tm)
    ho_ref[...] = h.astype(BF16)


def _g1(grp, a, w, x, mod, c_gate, g_post, mod_next, c_scale, c_shift, g_next, tm, name):
    k = a.shape[1]
    return pl.pallas_call(
        _g1_kernel,
        grid=(grp.rows // tm,),
        in_specs=[
            pl.BlockSpec((tm, k), lambda i: (i, 0)),
            pl.BlockSpec((k, D_MODEL), lambda i: (0, 0), pipeline_mode=pl.Buffered(1)),
            pl.BlockSpec((tm, D_MODEL), lambda i: (i, 0)),
            grp.mod_spec(tm, c_gate),
            _vec_spec(D_MODEL),
            _vec_spec(D_MODEL),
            grp.mod_spec(tm, c_scale),
            grp.mod_spec(tm, c_shift),
        ],
        out_specs=[pl.BlockSpec((tm, D_MODEL), lambda i: (i, 0)),
                   pl.BlockSpec((tm, D_MODEL), lambda i: (i, 0))],
        out_shape=[jax.ShapeDtypeStruct((grp.rows, D_MODEL), F32),
                   jax.ShapeDtypeStruct((grp.rows, D_MODEL), BF16)],
        compiler_params=_params(("arbitrary",)),
        name=name,
    )(a, w, x, grp.mod_array(mod), g_post.reshape(1, D_MODEL), g_next.reshape(1, D_MODEL),
      grp.mod_array(mod_next), grp.mod_array(mod_next))


def _ffn_act_p_kernel(gp_ref, val_ref, w_ref, b_ref, o_ref, carry, *, tm):
    i = pl.program_id(1)

    @pl.when(i == 0)
    def _():
        carry[...] = jnp.zeros(carry.shape, F32)

    gp = gp_ref[...].astype(F32)
    row = lax.broadcasted_iota(jnp.int32, (tm, 1), 0)
    prev1 = jnp.where(row == 0, carry[7:8, :], pltpu.roll(gp, 1, axis=0))
    prev2 = jnp.where(row == 0, carry[6:7, :],
                      jnp.where(row == 1, carry[7:8, :], pltpu.roll(gp, 2, axis=0)))
    gc = w_ref[0:1, :] * prev2 + w_ref[1:2, :] * prev1 + w_ref[2:3, :] * gp + b_ref[...]
    o_ref[...] = (_gelu(gc) * val_ref[...].astype(F32)).astype(BF16)
    carry[...] = gp[tm - 8:tm, :]


def _ffn_act_p(grp, upv, lw, tm):
    n_tiles = grp.t_len // tm
    full = lambda shape: pl.BlockSpec(shape, lambda n, i: (0,) * len(shape))
    return pl.pallas_call(
        functools.partial(_ffn_act_p_kernel, tm=tm),
        grid=(grp.n_seq, n_tiles),
        in_specs=[pl.BlockSpec((tm, D_FF), lambda n, i: (n * n_tiles + i, 0)),
                  pl.BlockSpec((tm, D_FF), lambda n, i: (n * n_tiles + i, 1)),
                  full((K_FFN, D_FF)), full((1, D_FF))],
        out_specs=pl.BlockSpec((tm, D_FF), lambda n, i: (n * n_tiles + i, 0)),
        out_shape=jax.ShapeDtypeStruct((grp.rows, D_FF), BF16),
        scratch_shapes=[pltpu.VMEM((8, D_FF), F32)],
        compiler_params=_params(("arbitrary", "arbitrary")),
        name="ffn_act_prompt",
    )(upv, upv, lw["w_ffn_conv"], lw["b_ffn_conv"])


def _ffn_act_s_kernel(gp_ref, val_ref, st_ref, w_ref, b_ref, o_ref, *, t_len, n_seq):
    hist = K_FFN - 1
    rows = [_hist_row(st_ref, j, hist, n_seq, D_FF) for j in range(hist)]
    rows += [gp_ref[t].astype(F32) for t in range(t_len)]
    for t in range(t_len):
        gc = b_ref[...] + w_ref[0:1, :] * rows[t]
        for k in range(1, K_FFN):
            gc = gc + w_ref[k:k + 1, :] * rows[t + k]
        o_ref[t] = (_gelu(gc) * val_ref[t].astype(F32)).astype(BF16)


def _ffn_act_s(grp, upv, state_ffn, lw, nb=32):
    t_len, n_seq = grp.t_len, grp.n_seq
    upv3 = upv.reshape(t_len, n_seq, 2 * D_FF)
    hist = K_FFN - 1
    return pl.pallas_call(
        functools.partial(_ffn_act_s_kernel, t_len=t_len, n_seq=nb),
        grid=(n_seq // nb,),
        in_specs=[pl.BlockSpec((t_len, nb, D_FF), lambda b: (0, b, 0)),
                  pl.BlockSpec((t_len, nb, D_FF), lambda b: (0, b, 1)),
                  pl.BlockSpec((nb * hist * D_FF // 128, 128), lambda b: (b, 0)),
                  pl.BlockSpec((K_FFN, D_FF), lambda b: (0, 0)),
                  pl.BlockSpec((1, D_FF), lambda b: (0, 0))],
        out_specs=pl.BlockSpec((t_len, nb, D_FF), lambda b: (0, b, 0)),
        out_shape=jax.ShapeDtypeStruct((t_len, n_seq, D_FF), BF16),
        compiler_params=_params(("arbitrary",)),
        name="ffn_act_sample",
    )(upv3, upv3, state_ffn.reshape(-1, 128), lw["w_ffn_conv"],
      lw["b_ffn_conv"]).reshape(grp.rows, D_FF)


SHIFT_M, SCALE_M, GATE_M, SHIFT_F, SCALE_F, GATE_F = range(6)


def _layer_weights(l, p):
    t_len = 4
    ws = p["w_spatial"][l]
    bs = p["b_spatial"][l]
    b_mix = jnp.repeat(bs.T, CHUNK, axis=1)
    w_vec = jnp.repeat(ws[:, :t_len, :t_len].transpose(1, 2, 0), CHUNK, axis=2)
    return dict(
        w_in=p["w_in"][l].astype(BF16), w_up=p["w_up"][l].astype(BF16),
        w_down=p["w_down"][l].astype(BF16), w_o=p["w_o"][l].astype(BF16),
        w_a_out=p["w_a_out"][l].astype(BF16), w_b_out=p["w_b_out"][l].astype(BF16),
        w_pool_grp=p["w_pool_grp"][l].astype(BF16),
        pool_scale=p["pool_scale"][l].reshape(1, D_MODEL),
        b_gate=p["b_gate"][l].reshape(1, 3 * D_MODEL),
        ln_v_g=p["ln_v_g"][l].reshape(1, D_A), ln_v_b=p["ln_v_b"][l].reshape(1, D_A),
        w_spatial=ws, b_mix=b_mix,
        w_spatial_vec=w_vec.reshape(t_len * t_len, D_A), b_spatial_vec=b_mix[:t_len],
        w_dwconv=p["w_dwconv"][l], b_dwconv=p["b_dwconv"][l].reshape(1, D_B),
        ln_conv_g=p["ln_conv_g"][l].reshape(1, D_B), ln_conv_b=p["ln_conv_b"][l].reshape(1, D_B),
        w_ffn_conv=p["w_ffn_conv"][l], b_ffn_conv=p["b_ffn_conv"][l].reshape(1, D_FF),
    )


def kernel(x_prompt, x_sample, c_prompt, c_sample, state_conv, state_pool, state_ffn_conv, ada_w, ada_b, g_pre_mix, g_post_mix, g_pre_ffn, g_post_ffn, w_in, b_gate, ln_v_g, ln_v_b, w_spatial, b_spatial, w_a_out, w_dwconv, b_dwconv, ln_conv_g, ln_conv_b, w_b_out, w_pool_grp, pool_scale, w_o, w_up, w_ffn_conv, b_ffn_conv, w_down):
    p = dict(ada_w=ada_w, ada_b=ada_b, w_in=w_in, b_gate=b_gate, ln_v_g=ln_v_g, ln_v_b=ln_v_b,
             w_spatial=w_spatial, b_spatial=b_spatial, w_a_out=w_a_out, w_dwconv=w_dwconv,
             b_dwconv=b_dwconv, ln_conv_g=ln_conv_g, ln_conv_b=ln_conv_b, w_b_out=w_b_out,
             w_pool_grp=w_pool_grp, pool_scale=pool_scale, w_o=w_o, w_up=w_up,
             w_ffn_conv=w_ffn_conv, b_ffn_conv=b_ffn_conv, w_down=w_down)
    n_p, t_p, _ = x_prompt.shape
    n_s, t_s, _ = x_sample.shape
    gp_ = _Group(n_p, t_p, time_major=False)
    gs_ = _Group(n_s, t_s, time_major=True)

    pad = (-(n_p + n_s)) % 8
    c_all = jnp.concatenate([c_prompt, c_sample, jnp.zeros((pad, D_MODEL), F32)], axis=0)
    mods = _mods(c_all, ada_w, ada_b)
    mod_p = [mods[l, :n_p] for l in range(DEPTH)]
    mod_s = [mods[l, n_p:n_p + n_s] for l in range(DEPTH)]

    x_p = x_prompt.reshape(n_p * t_p, D_MODEL)
    x_s = x_sample.transpose(1, 0, 2).reshape(t_s * n_s, D_MODEL)
    tm_p, tm_s = 512, t_s * n_s
    tm_seq = 256

    h_p = _prenorm(gp_, x_p, g_pre_mix[0], mod_p[0], SCALE_M, SHIFT_M, tm_p)
    h_s = _prenorm(gs_, x_s, g_pre_mix[0], mod_s[0], SCALE_M, SHIFT_M, tm_s)

    outs = {k: [] for k in ("conv_p", "conv_s", "pool_p", "pool_s", "ffn_p", "ffn_s", "v_s")}
    for l in range(DEPTH):
        lw = _layer_weights(l, p)
        nxt = min(l + 1, DEPTH - 1)

        proj_p = _mm(h_p, lw["w_in"], tm_p, 1024, "proj_prompt")
        proj_s = _mm(h_s, lw["w_in"], tm_s, 1024, "proj_sample")
        acts_p, nconv_p = _seq_p(gp_, proj_p, lw, tm_seq)
        acts_s, vn_s, xb_s = _seq_s(gs_, proj_s, state_conv[l], state_pool[l], lw)
        mix_p = _mixmm(acts_p, proj_p, lw, tm_seq)
        mix_s = _mixmm(acts_s.reshape(gs_.rows, 3 * 1024), proj_s, lw, tm_seq)
        x_p, h_p = _g1(gp_, mix_p, lw["w_o"], x_p, mod_p[l], GATE_M, g_post_mix[l],
                       mod_p[l], SCALE_F, SHIFT_F, g_pre_ffn[l], tm_seq, "wo_prompt")
        x_s, h_s = _g1(gs_, mix_s, lw["w_o"], x_s, mod_s[l], GATE_M, g_post_mix[l],
                       mod_s[l], SCALE_F, SHIFT_F, g_pre_ffn[l], tm_s, "wo_sample")

        upv_p = _mm(h_p, lw["w_up"], tm_p, 1024, "up_prompt")
        upv_s = _mm(h_s, lw["w_up"], tm_s, 1024, "up_sample")
        act_p = _ffn_act_p(gp_, upv_p, lw, tm_seq)
        act_s = _ffn_act_s(gs_, upv_s, state_ffn_conv[l], lw)
        x_p, h_p = _g1(gp_, act_p, lw["w_down"], x_p, mod_p[l], GATE_F, g_post_ffn[l],
                       mod_p[nxt], SCALE_M, SHIFT_M, g_pre_mix[nxt], tm_seq, "down_prompt")
        x_s, h_s = _g1(gs_, act_s, lw["w_down"], x_s, mod_s[l], GATE_F, g_post_ffn[l],
                       mod_s[nxt], SCALE_M, SHIFT_M, g_pre_mix[nxt], tm_s, "down_sample")

        proj_p3 = proj_p.reshape(n_p, t_p, N_IN)
        proj_s3 = proj_s.reshape(t_s, n_s, N_IN)
        pin_lo = COL_PIN * 1024
        outs["conv_p"].append(nconv_p)
        outs["conv_s"].append(jnp.concatenate(
            [state_conv[l][:, t_s:], xb_s.transpose(1, 0, 2)], axis=1))
        outs["pool_p"].append(proj_p3[:, t_p - POOL_BUF:, pin_lo:pin_lo + D_C].astype(F32))
        outs["pool_s"].append(jnp.concatenate(
            [state_pool[l][:, t_s:],
             proj_s3[:, :, pin_lo:pin_lo + D_C].astype(F32).transpose(1, 0, 2)], axis=1))
        outs["ffn_p"].append(
            upv_p.reshape(n_p, t_p, 2 * D_FF)[:, t_p - (K_FFN - 1):, :D_FF].astype(F32))
        outs["ffn_s"].append(
            upv_s.reshape(t_s, n_s, 2 * D_FF)[t_s - (K_FFN - 1):, :, :D_FF].astype(F32)
            .transpose(1, 0, 2))
        outs["v_s"].append(vn_s.transpose(1, 0, 2))

    y_p = x_p.reshape(n_p, t_p, D_MODEL)
    y_s = x_s.reshape(t_s, n_s, D_MODEL).transpose(1, 0, 2)
    return (y_p, y_s, jnp.stack(outs["conv_p"]), jnp.stack(outs["conv_s"]),
            jnp.stack(outs["pool_p"]), jnp.stack(outs["pool_s"]), jnp.stack(outs["ffn_p"]),
            jnp.stack(outs["ffn_s"]), jnp.stack(outs["v_s"]))
```

```python
import functools

import jax
import jax.numpy as jnp
from jax import lax
from jax.experimental import pallas as pl
from jax.experimental.pallas import tpu as pltpu

F32 = jnp.float32
BF16 = jnp.bfloat16

D_MODEL = 2048
DEPTH = 4
PAST_LEN = 16384
CHUNK = 128
D_A = D_MODEL // 2
D_B = D_MODEL // 2
D_C = D_MODEL // 2
G_A = D_A // CHUNK
K_CONV = 31
POOL_WINDOWS = (2, 4, 8, 16)
N_POOL = len(POOL_WINDOWS)
D_CG = D_C // N_POOL
D_CG_OUT = D_MODEL // N_POOL
POOL_BUF = max(POOL_WINDOWS) - 1
K_FFN = 3
D_FF = 5504
N_IN = 2 * D_A + 2 * D_B + D_C + 3 * D_MODEL
EPS = 1e-6

LANES = 128
SUBLANES = 8
ROW_VREGS = D_B // LANES

COL_U, COL_V, COL_GA, COL_GB, COL_PIN, COL_GATES = 0, 1, 2, 3, 4, 5

CONV_HALO = 32
POOL_HALO = 16
SEQ_TB = 8

VMEM_LIMIT = 56 * 1024 * 1024


def _params(sem):
    return pltpu.CompilerParams(dimension_semantics=sem, vmem_limit_bytes=VMEM_LIMIT)


def _gelu(x):
    return jax.nn.gelu(x, approximate=True)


def _rmsnorm(x, g):
    return x * lax.rsqrt(jnp.mean(x * x, axis=-1, keepdims=True) + EPS) * g


def _layernorm(x, g, b):
    mu = jnp.mean(x, axis=-1, keepdims=True)
    xc = x - mu
    var = jnp.mean(xc * xc, axis=-1, keepdims=True)
    return xc * lax.rsqrt(var + EPS) * g + b


def _rows(m, tm):
    rm, c = m.shape
    if rm == 1 or rm == tm:
        return m
    return jnp.broadcast_to(m[None], (tm // rm, rm, c)).reshape(tm, c)


def _resident(shape, index_map):
    return pl.BlockSpec(shape, index_map, pipeline_mode=pl.Buffered(1))


def _mods_kernel(c_ref, w_ref, b_ref, o_ref):
    c = c_ref[...]
    a = (c * jax.nn.sigmoid(c)).astype(BF16)
    o_ref[...] = jnp.dot(a, w_ref[...].astype(BF16), preferred_element_type=F32) + b_ref[...]


def _mods(c_all, ada_w, ada_b, tn=1024):
    rows = c_all.shape[0]
    n_out = ada_w.shape[-1]
    return pl.pallas_call(
        _mods_kernel,
        grid=(DEPTH, n_out // tn),
        in_specs=[
            pl.BlockSpec((rows, D_MODEL), lambda l, j: (0, 0)),
            pl.BlockSpec((None, D_MODEL, tn), lambda l, j: (l, 0, j)),
            pl.BlockSpec((None, 1, tn), lambda l, j: (l, 0, j)),
        ],
        out_specs=pl.BlockSpec((None, rows, tn), lambda l, j: (l, 0, j)),
        out_shape=jax.ShapeDtypeStruct((DEPTH, rows, n_out), F32),
        compiler_params=_params(("arbitrary", "arbitrary")),
        name="mods",
    )(c_all, ada_w, ada_b.reshape(DEPTH, 1, n_out))


class _Group:
    def __init__(self, n_seq, t_len, time_major, mods):
        self.n_seq = n_seq
        self.t_len = t_len
        self.time_major = time_major
        self.rows = n_seq * t_len
        self.mods = mods

    def mod_spec(self, tm, layer, chunk):
        if self.time_major:
            return pl.BlockSpec((None, self.n_seq, D_MODEL), lambda i: (layer, 0, chunk))
        per_seq = self.t_len // tm
        return pl.BlockSpec((None, None, 1, D_MODEL), lambda i: (layer, i // per_seq, 0, chunk))


def _vec_spec(width, layer):
    return pl.BlockSpec((None, 1, width), lambda *_: (layer, 0, 0))


def _vec(p):
    return p.reshape(DEPTH, 1, p.shape[-1])


def _prenorm_kernel(x_ref, g_ref, sc_ref, sh_ref, h_ref):
    x = x_ref[...]
    tm = x.shape[0]
    y = _rmsnorm(x, g_ref[...])
    h_ref[...] = (y * (1.0 + _rows(sc_ref[...], tm)) + _rows(sh_ref[...], tm)).astype(BF16)


def _prenorm(grp, x, g, layer, c_scale, c_shift, tm):
    return pl.pallas_call(
        _prenorm_kernel,
        grid=(grp.rows // tm,),
        in_specs=[
            pl.BlockSpec((tm, D_MODEL), lambda i: (i, 0)),
            _vec_spec(D_MODEL, layer),
            grp.mod_spec(tm, layer, c_scale),
            grp.mod_spec(tm, layer, c_shift),
        ],
        out_specs=pl.BlockSpec((tm, D_MODEL), lambda i: (i, 0)),
        out_shape=jax.ShapeDtypeStruct((grp.rows, D_MODEL), BF16),
        compiler_params=_params(("arbitrary",)),
        name="prenorm",
    )(x, _vec(g), grp.mods, grp.mods)


def _mm_kernel(a_ref, w_ref, o_ref, wb):
    @pl.when(pl.program_id(1) == 0)
    def _():
        wb[...] = w_ref[...].astype(BF16)

    o_ref[...] = jnp.dot(a_ref[...], wb[...], preferred_element_type=F32).astype(o_ref.dtype)


def _mm(a, w, layer, tm, tn, name):
    m, k = a.shape
    n = w.shape[-1]
    return pl.pallas_call(
        _mm_kernel,
        grid=(pl.cdiv(n, tn), m // tm),
        in_specs=[
            pl.BlockSpec((tm, k), lambda j, i: (i, 0)),
            pl.BlockSpec((None, k, tn), lambda j, i: (layer, 0, j)),
        ],
        out_specs=pl.BlockSpec((tm, tn), lambda j, i: (i, j)),
        out_shape=jax.ShapeDtypeStruct((m, n), BF16),
        scratch_shapes=[pltpu.VMEM((k, tn), BF16)],
        compiler_params=_params(("arbitrary", "arbitrary")),
        name=name,
    )(a, w)


def _to_rowvreg(ref, row0, x):
    tm = x.shape[0]
    for c in range(ROW_VREGS):
        ref[pl.ds(row0 * ROW_VREGS + c, tm, stride=ROW_VREGS), :] = x[:, c * LANES:(c + 1) * LANES]


def _from_rowvreg(ref, tm):
    return jnp.concatenate(
        [ref[pl.ds(c, tm, stride=ROW_VREGS), :] for c in range(ROW_VREGS)], axis=-1)


def _seq_p_kernel(u_ref, v_ref, ga_ref, gb_ref, pin_ref, lnvg_ref, lnvb_ref, ws_ref, bmix_ref,
                  wconv_ref, bconv_ref, lncg_ref, lncb_ref,
                  acts_ref, nconv_ref, xrv, yrv, prv, qrv, *, tm, n_tiles, start):
    i = pl.program_id(1)
    rv = ROW_VREGS

    vn = _layernorm(_gelu(v_ref[...].astype(F32)), lnvg_ref[...], lnvb_ref[...]).astype(BF16)
    gu = _gelu(u_ref[...].astype(F32))
    row = lax.broadcasted_iota(jnp.int32, (CHUNK, CHUNK), 0)
    col = lax.broadcasted_iota(jnp.int32, (CHUNK, CHUNK), 1)
    for g in range(G_A):
        lanes = slice(g * CHUNK, (g + 1) * CHUNK)
        wg = jnp.where(row >= col, ws_ref[g], 0.0).astype(BF16)
        for c in range(tm // CHUNK):
            rws = slice(c * CHUNK, (c + 1) * CHUNK)
            mixed = jnp.dot(wg, vn[rws, lanes], preferred_element_type=F32) + bmix_ref[:, lanes]
            acts_ref[rws, lanes] = (gu[rws, lanes] * mixed).astype(BF16)

    @pl.when(i == 0)
    def _():
        xrv[0:CONV_HALO * rv, :] = jnp.zeros((CONV_HALO * rv, LANES), F32)
        prv[0:POOL_HALO * rv, :] = jnp.zeros((POOL_HALO * rv, LANES), F32)

    @pl.when(i > 0)
    def _():
        xrv[0:CONV_HALO * rv, :] = xrv[tm * rv:(tm + CONV_HALO) * rv, :]
        prv[0:POOL_HALO * rv, :] = prv[tm * rv:(tm + POOL_HALO) * rv, :]

    xb = ga_ref[...].astype(F32) * jax.nn.sigmoid(gb_ref[...].astype(F32))
    _to_rowvreg(xrv, CONV_HALO, xb)

    @pl.when(i == n_tiles - 1)
    def _():
        nconv_ref[...] = xb[tm - (K_CONV - 1):tm, :]

    taps = [wconv_ref[k * rv:(k + 1) * rv, :] for k in range(K_CONV)]
    bias = bconv_ref[...]
    first = CONV_HALO - (K_CONV - 1)

    def conv_body(it, carry):
        t0 = it * SEQ_TB
        base = pl.multiple_of((t0 + first) * rv, rv)
        acc = [bias] * SEQ_TB
        for j in range(SEQ_TB + K_CONV - 1):
            xj = xrv[pl.ds(base + j * rv, rv), :]
            for tt in range(SEQ_TB):
                k = j - tt
                if 0 <= k < K_CONV:
                    acc[tt] = acc[tt] + taps[k] * xj
        for tt in range(SEQ_TB):
            yrv[pl.ds(pl.multiple_of((t0 + tt) * rv, rv), rv), :] = acc[tt]
        return carry

    lax.fori_loop(0, tm // SEQ_TB, conv_body, 0)
    zb = _layernorm(_from_rowvreg(yrv, tm), lncg_ref[...], lncb_ref[...])
    acts_ref[:, D_A:D_A + D_B] = (zb * jax.nn.sigmoid(zb)).astype(BF16)

    _to_rowvreg(prv, POOL_HALO, pin_ref[...].astype(F32))
    grp_of_sublane = lax.broadcasted_iota(jnp.int32, (rv, LANES), 0) // (rv // N_POOL)
    window = jnp.left_shift(POOL_WINDOWS[0], grp_of_sublane)
    pos0 = start + i * tm

    def pool_body(it, carry):
        t0 = it * SEQ_TB
        base = pl.multiple_of((t0 + POOL_HALO - POOL_BUF) * rv, rv)
        x = [prv[pl.ds(base + j * rv, rv), :] for j in range(SEQ_TB + POOL_BUF)]
        for tt in range(SEQ_TB):
            cur = POOL_BUF + tt
            s = x[cur] + x[cur - 1]
            lo = 2
            for gi in range(1, N_POOL):
                hi = POOL_WINDOWS[gi]
                part = x[cur - lo]
                for j in range(lo + 1, hi):
                    part = part + x[cur - j]
                s = s + jnp.where(grp_of_sublane >= gi, part, 0.0)
                lo = hi
            cnt = jnp.minimum(pos0 + t0 + tt + 1, window).astype(F32)
            qrv[pl.ds(pl.multiple_of((t0 + tt) * rv, rv), rv), :] = s / cnt - x[cur]
        return carry

    lax.fori_loop(0, tm // SEQ_TB, pool_body, 0)
    acts_ref[:, D_A + D_B:D_A + D_B + D_C] = _from_rowvreg(qrv, tm).astype(BF16)


def _seq_p(grp, proj, p, layer, tm):
    n_tiles = grp.t_len // tm
    rv = ROW_VREGS

    def col(c):
        return pl.BlockSpec((tm, 1024), lambda n, i: (n * n_tiles + i, c))

    def lay(*shape):
        return pl.BlockSpec((None,) + shape, lambda n, i: (layer,) + (0,) * len(shape))

    return pl.pallas_call(
        functools.partial(_seq_p_kernel, tm=tm, n_tiles=n_tiles, start=0),
        grid=(grp.n_seq, n_tiles),
        in_specs=[col(COL_U), col(COL_V), col(COL_GA), col(COL_GB), col(COL_PIN),
                  lay(1, D_A), lay(1, D_A), lay(G_A, CHUNK, CHUNK), lay(CHUNK, D_A),
                  lay(K_CONV * rv, LANES), lay(rv, LANES), lay(1, D_B), lay(1, D_B)],
        out_specs=[pl.BlockSpec((tm, 3 * 1024), lambda n, i: (n * n_tiles + i, 0)),
                   pl.BlockSpec((None, K_CONV - 1, D_B), lambda n, i: (n, 0, 0))],
        out_shape=[jax.ShapeDtypeStruct((grp.rows, 3 * 1024), BF16),
                   jax.ShapeDtypeStruct((grp.n_seq, K_CONV - 1, D_B), F32)],
        scratch_shapes=[pltpu.VMEM(((CONV_HALO + tm) * rv, LANES), F32),
                        pltpu.VMEM((tm * rv, LANES), F32),
                        pltpu.VMEM(((POOL_HALO + tm) * rv, LANES), F32),
                        pltpu.VMEM((tm * rv, LANES), F32)],
        compiler_params=_params(("arbitrary", "arbitrary")),
        name="seq_prompt",
    )(proj, proj, proj, proj, proj, p["ln_v_g"], p["ln_v_b"], p["w_spatial"], p["b_mix"],
      p["w_dwconv_rv"], p["b_dwconv_rv"], p["ln_conv_g"], p["ln_conv_b"])


def _seq_s_kernel(u_ref, v_ref, ga_ref, gb_ref, pin_ref, sconv_ref, spool_ref,
                  lnvg_ref, lnvb_ref, wv_ref, bv_ref, wconv_ref, bconv_ref, lncg_ref, lncb_ref,
                  acts_ref, vn_ref, xb_ref, *, t_len, nb, start):
    vn = []
    for t in range(t_len):
        vn_t = _layernorm(_gelu(v_ref[t].astype(F32)), lnvg_ref[...], lnvb_ref[...])
        vn_ref[t] = vn_t
        vn.append(vn_t)
        mixed = jnp.broadcast_to(bv_ref[t:t + 1, :], (nb, D_A))
        for s in range(t + 1):
            mixed = mixed + wv_ref[t * t_len + s:t * t_len + s + 1, :] * vn[s]
        acts_ref[t, :, 0:D_A] = (_gelu(u_ref[t].astype(F32)) * mixed).astype(BF16)

    hist = K_CONV - 1
    acc = [jnp.broadcast_to(bconv_ref[...], (nb, D_B)) for _ in range(t_len)]
    for j in range(hist + t_len):
        if j < hist:
            xj = sconv_ref[j]
        else:
            xj = ga_ref[j - hist].astype(F32) * jax.nn.sigmoid(gb_ref[j - hist].astype(F32))
            xb_ref[j - hist] = xj
        for t in range(t_len):
            k = j - t
            if 0 <= k < K_CONV:
                acc[t] = acc[t] + wconv_ref[k:k + 1, :] * xj
    for t in range(t_len):
        zb = _layernorm(acc[t], lncg_ref[...], lncb_ref[...])
        acts_ref[t, :, D_A:D_A + D_B] = (zb * jax.nn.sigmoid(zb)).astype(BF16)

    rows = [spool_ref[j] for j in range(POOL_BUF)]
    rows += [pin_ref[t].astype(F32) for t in range(t_len)]
    for t in range(t_len):
        for gi, w in enumerate(POOL_WINDOWS):
            lanes = slice(gi * D_CG, (gi + 1) * D_CG)
            s = rows[POOL_BUF + t][:, lanes]
            for j in range(1, w):
                s = s + rows[POOL_BUF + t - j][:, lanes]
            cnt = float(min(start + t + 1, w))
            acts_ref[t, :, D_A + D_B + gi * D_CG:D_A + D_B + (gi + 1) * D_CG] = (
                s / cnt - rows[POOL_BUF + t][:, lanes]).astype(BF16)


def _seq_s(grp, proj, sconv_t, spool_t, p, layer, nb=32):
    t_len, n_seq = grp.t_len, grp.n_seq
    proj3 = proj.reshape(t_len, n_seq, N_IN)
    hist = K_CONV - 1

    def col(c):
        return pl.BlockSpec((t_len, nb, 1024), lambda b: (0, b, c))

    def lay(*shape):
        return pl.BlockSpec((None,) + shape, lambda b: (layer,) + (0,) * len(shape))

    return pl.pallas_call(
        functools.partial(_seq_s_kernel, t_len=t_len, nb=nb, start=PAST_LEN),
        grid=(n_seq // nb,),
        in_specs=[col(COL_U), col(COL_V), col(COL_GA), col(COL_GB), col(COL_PIN),
                  pl.BlockSpec((None, hist, nb, D_B), lambda b: (layer, 0, b, 0)),
                  pl.BlockSpec((None, POOL_BUF, nb, D_C), lambda b: (layer, 0, b, 0)),
                  lay(1, D_A), lay(1, D_A), lay(t_len * t_len, D_A), lay(t_len, D_A),
                  lay(K_CONV, D_B), lay(1, D_B), lay(1, D_B), lay(1, D_B)],
        out_specs=[pl.BlockSpec((t_len, nb, 3 * 1024), lambda b: (0, b, 0)),
                   pl.BlockSpec((t_len, nb, D_A), lambda b: (0, b, 0)),
                   pl.BlockSpec((t_len, nb, D_B), lambda b: (0, b, 0))],
        out_shape=[jax.ShapeDtypeStruct((t_len, n_seq, 3 * 1024), BF16),
                   jax.ShapeDtypeStruct((t_len, n_seq, D_A), F32),
                   jax.ShapeDtypeStruct((t_len, n_seq, D_B), F32)],
        compiler_params=_params(("arbitrary",)),
        name="seq_sample",
    )(proj3, proj3, proj3, proj3, proj3, sconv_t, spool_t,
      p["ln_v_g"], p["ln_v_b"], p["w_spatial_vec"], p["b_spatial_vec"],
      p["w_dwconv"], p["b_dwconv"], p["ln_conv_g"], p["ln_conv_b"])


def _mixmm_kernel(acts_ref, g0, g1, g2, g3, g4, g5, bgate_ref, wa_ref, wb_ref, wp_ref, ps_ref,
                  o_ref, wa_s, wb_s, wp_s):
    @pl.when(pl.program_id(0) == 0)
    def _():
        wa_s[...] = wa_ref[...].astype(BF16)
        wb_s[...] = wb_ref[...].astype(BF16)
        wp_s[...] = wp_ref[...].astype(BF16)

    gate_refs = (g0, g1, g2, g3, g4, g5)

    def gate(branch, half):
        idx = 2 * branch + half
        pre = gate_refs[idx][...].astype(F32) + bgate_ref[:, idx * 1024:(idx + 1) * 1024]
        return jax.nn.sigmoid(pre)

    ya = jnp.dot(acts_ref[:, 0:D_A], wa_s[...], preferred_element_type=F32)
    yb = jnp.dot(acts_ref[:, D_A:D_A + D_B], wb_s[...], preferred_element_type=F32)
    for half in range(2):
        cols = slice(half * 1024, (half + 1) * 1024)
        yc = jnp.concatenate(
            [jnp.dot(acts_ref[:, D_A + D_B + g * D_CG:D_A + D_B + (g + 1) * D_CG], wp_s[g],
                     preferred_element_type=F32) for g in (2 * half, 2 * half + 1)],
            axis=-1) * ps_ref[:, cols]
        mix = gate(0, half) * ya[:, cols] + gate(1, half) * yb[:, cols] + gate(2, half) * yc
        o_ref[:, cols] = mix.astype(BF16)


def _mixmm(acts, proj, p, layer, tm):
    m = acts.shape[0]

    def gcol(c):
        return pl.BlockSpec((tm, 1024), lambda i: (i, COL_GATES + c))

    def lay(*shape):
        return _resident((None,) + shape, lambda i: (layer,) + (0,) * len(shape))

    return pl.pallas_call(
        _mixmm_kernel,
        grid=(m // tm,),
        in_specs=[pl.BlockSpec((tm, 3 * 1024), lambda i: (i, 0))] + [gcol(c) for c in range(6)] + [
            lay(1, 3 * D_MODEL), lay(D_A, D_MODEL), lay(D_B, D_MODEL),
            lay(N_POOL, D_CG, D_CG_OUT), lay(1, D_MODEL)],
        out_specs=pl.BlockSpec((tm, D_MODEL), lambda i: (i, 0)),
        out_shape=jax.ShapeDtypeStruct((m, D_MODEL), BF16),
        scratch_shapes=[pltpu.VMEM((D_A, D_MODEL), BF16), pltpu.VMEM((D_B, D_MODEL), BF16),
                        pltpu.VMEM((N_POOL, D_CG, D_CG_OUT), BF16)],
        compiler_params=_params(("arbitrary",)),
        name="mixmm",
    )(acts, proj, proj, proj, proj, proj, proj, p["b_gate"], p["w_a_out"], p["w_b_out"],
      p["w_pool_grp"], p["pool_scale"])


def _g1_kernel(a_ref, w_ref, x_ref, gate_ref, gpost_ref, gnext_ref, sc_ref, sh_ref,
               xo_ref, ho_ref, *scratch):
    tm = x_ref.shape[0]
    if scratch:
        wb, = scratch

        @pl.when(pl.program_id(0) == 0)
        def _():
            wb[...] = w_ref[...].astype(BF16)

        w = wb[...]
    else:
        w = w_ref[...]
    y = jnp.dot(a_ref[...], w, preferred_element_type=F32)
    x = x_ref[...] + _rows(gate_ref[...], tm) * _rmsnorm(y, gpost_ref[...])
    xo_ref[...] = x
    h = _rmsnorm(x, gnext_ref[...]) * (1.0 + _rows(sc_ref[...], tm)) + _rows(sh_ref[...], tm)
    ho_ref[...] = h.astype(BF16)


def _g1(grp, a, w, x, layer, c_gate, g_post, layer_next, c_scale, c_shift, g_next, tm, name):
    k = a.shape[1]
    scratch = [pltpu.VMEM((k, D_MODEL), BF16)] if w.dtype == F32 else []
    return pl.pallas_call(
        _g1_kernel,
        grid=(grp.rows // tm,),
        in_specs=[
            pl.BlockSpec((tm, k), lambda i: (i, 0)),
            _resident((None, k, D_MODEL), lambda i: (layer, 0, 0)),
            pl.BlockSpec((tm, D_MODEL), lambda i: (i, 0)),
            grp.mod_spec(tm, layer, c_gate),
            _vec_spec(D_MODEL, layer),
            _vec_spec(D_MODEL, layer_next),
            grp.mod_spec(tm, layer_next, c_scale),
            grp.mod_spec(tm, layer_next, c_shift),
        ],
        out_specs=[pl.BlockSpec((tm, D_MODEL), lambda i: (i, 0)),
                   pl.BlockSpec((tm, D_MODEL), lambda i: (i, 0))],
        out_shape=[jax.ShapeDtypeStruct((grp.rows, D_MODEL), F32),
                   jax.ShapeDtypeStruct((grp.rows, D_MODEL), BF16)],
        scratch_shapes=scratch,
        compiler_params=_params(("arbitrary",)),
        name=name,
    )(a, w, x, grp.mods, _vec(g_post), _vec(g_next), grp.mods, grp.mods)


def _ffn_act_p_kernel(gp_ref, val_ref, w_ref, b_ref, o_ref, carry, *, tm):
    i = pl.program_id(1)

    @pl.when(i == 0)
    def _():
        carry[...] = jnp.zeros(carry.shape, F32)

    gp = gp_ref[...].astype(F32)
    row = lax.broadcasted_iota(jnp.int32, (tm, 1), 0)
    prev1 = jnp.where(row == 0, carry[7:8, :], pltpu.roll(gp, 1, axis=0))
    prev2 = jnp.where(row == 0, carry[6:7, :],
                      jnp.where(row == 1, carry[7:8, :], pltpu.roll(gp, 2, axis=0)))
    gc = w_ref[0:1, :] * prev2 + w_ref[1:2, :] * prev1 + w_ref[2:3, :] * gp + b_ref[...]
    o_ref[...] = (_gelu(gc) * val_ref[...].astype(F32)).astype(BF16)
    carry[...] = gp[tm - 8:tm, :]


def _ffn_act_p(grp, upv, p, layer, tm):
    n_tiles = grp.t_len // tm

    def lay(*shape):
        return pl.BlockSpec((None,) + shape, lambda n, i: (layer,) + (0,) * len(shape))

    return pl.pallas_call(
        functools.partial(_ffn_act_p_kernel, tm=tm),
        grid=(grp.n_seq, n_tiles),
        in_specs=[pl.BlockSpec((tm, D_FF), lambda n, i: (n * n_tiles + i, 0)),
                  pl.BlockSpec((tm, D_FF), lambda n, i: (n * n_tiles + i, 1)),
                  lay(K_FFN, D_FF), lay(1, D_FF)],
        out_specs=pl.BlockSpec((tm, D_FF), lambda n, i: (n * n_tiles + i, 0)),
        out_shape=jax.ShapeDtypeStruct((grp.rows, D_FF), BF16),
        scratch_shapes=[pltpu.VMEM((8, D_FF), F32)],
        compiler_params=_params(("arbitrary", "arbitrary")),
        name="ffn_act_prompt",
    )(upv, upv, p["w_ffn_conv"], p["b_ffn_conv"])


def _ffn_act_s_kernel(gp_ref, val_ref, st_ref, w_ref, b_ref, o_ref, *, t_len):
    hist = K_FFN - 1
    rows = [st_ref[j] for j in range(hist)]
    rows += [gp_ref[t].astype(F32) for t in range(t_len)]
    for t in range(t_len):
        gc = b_ref[...] + w_ref[0:1, :] * rows[t]
        for k in range(1, K_FFN):
            gc = gc + w_ref[k:k + 1, :] * rows[t + k]
        o_ref[t] = (_gelu(gc) * val_ref[t].astype(F32)).astype(BF16)


def _ffn_act_s(grp, upv, sffn_t, p, layer, nb=32):
    t_len, n_seq = grp.t_len, grp.n_seq
    upv3 = upv.reshape(t_len, n_seq, 2 * D_FF)
    hist = K_FFN - 1

    def lay(*shape):
        return pl.BlockSpec((None,) + shape, lambda b: (layer,) + (0,) * len(shape))

    return pl.pallas_call(
        functools.partial(_ffn_act_s_kernel, t_len=t_len),
        grid=(n_seq // nb,),
        in_specs=[pl.BlockSpec((t_len, nb, D_FF), lambda b: (0, b, 0)),
                  pl.BlockSpec((t_len, nb, D_FF), lambda b: (0, b, 1)),
                  pl.BlockSpec((None, hist, nb, D_FF), lambda b: (layer, 0, b, 0)),
                  lay(K_FFN, D_FF), lay(1, D_FF)],
        out_specs=pl.BlockSpec((t_len, nb, D_FF), lambda b: (0, b, 0)),
        out_shape=jax.ShapeDtypeStruct((t_len, n_seq, D_FF), BF16),
        compiler_params=_params(("arbitrary",)),
        name="ffn_act_sample",
    )(upv3, upv3, sffn_t, p["w_ffn_conv"], p["b_ffn_conv"]).reshape(grp.rows, D_FF)


SHIFT_M, SCALE_M, GATE_M, SHIFT_F, SCALE_F, GATE_F = range(6)


def _small_params(p, t_s):
    ws, bs = p["w_spatial"], p["b_spatial"]
    b_mix = jnp.repeat(bs.transpose(0, 2, 1), CHUNK, axis=2)
    w_vec = jnp.repeat(ws[:, :, :t_s, :t_s].transpose(0, 2, 3, 1), CHUNK, axis=3)
    return dict(
        pool_scale=_vec(p["pool_scale"]), b_gate=_vec(p["b_gate"]),
        ln_v_g=_vec(p["ln_v_g"]), ln_v_b=_vec(p["ln_v_b"]),
        w_spatial=ws, b_mix=b_mix,
        w_spatial_vec=w_vec.reshape(DEPTH, t_s * t_s, D_A), b_spatial_vec=b_mix[:, :t_s],
        w_dwconv=p["w_dwconv"], b_dwconv=_vec(p["b_dwconv"]),
        w_dwconv_rv=p["w_dwconv"].reshape(DEPTH, K_CONV * ROW_VREGS, LANES),
        b_dwconv_rv=p["b_dwconv"].reshape(DEPTH, ROW_VREGS, LANES),
        ln_conv_g=_vec(p["ln_conv_g"]), ln_conv_b=_vec(p["ln_conv_b"]),
        w_ffn_conv=p["w_ffn_conv"], b_ffn_conv=_vec(p["b_ffn_conv"]),
        w_a_out=p["w_a_out"], w_b_out=p["w_b_out"], w_pool_grp=p["w_pool_grp"],
    )


def kernel(x_prompt, x_sample, c_prompt, c_sample, state_conv, state_pool, state_ffn_conv, ada_w, ada_b, g_pre_mix, g_post_mix, g_pre_ffn, g_post_ffn, w_in, b_gate, ln_v_g, ln_v_b, w_spatial, b_spatial, w_a_out, w_dwconv, b_dwconv, ln_conv_g, ln_conv_b, w_b_out, w_pool_grp, pool_scale, w_o, w_up, w_ffn_conv, b_ffn_conv, w_down):
    n_p, t_p, _ = x_prompt.shape
    n_s, t_s, _ = x_sample.shape
    sp = _small_params(dict(
        b_gate=b_gate, ln_v_g=ln_v_g, ln_v_b=ln_v_b, w_spatial=w_spatial, b_spatial=b_spatial,
        w_a_out=w_a_out, w_dwconv=w_dwconv, b_dwconv=b_dwconv, ln_conv_g=ln_conv_g,
        ln_conv_b=ln_conv_b, w_b_out=w_b_out, w_pool_grp=w_pool_grp, pool_scale=pool_scale,
        w_ffn_conv=w_ffn_conv, b_ffn_conv=b_ffn_conv), t_s)
    w_down_bf = w_down.astype(BF16)

    pad = (-(n_p + n_s)) % SUBLANES
    c_all = jnp.concatenate([c_sample, c_prompt, jnp.zeros((pad, D_MODEL), F32)], axis=0)
    mods = _mods(c_all, ada_w, ada_b)
    gs_ = _Group(n_s, t_s, True, mods)
    gp_ = _Group(n_p, t_p, False, mods[:, n_s:n_s + n_p].reshape(DEPTH, n_p, 1, 6 * D_MODEL))

    sconv_t = state_conv.transpose(0, 2, 1, 3)
    spool_t = state_pool.transpose(0, 2, 1, 3)
    sffn_t = state_ffn_conv.transpose(0, 2, 1, 3)

    x_p = x_prompt.reshape(n_p * t_p, D_MODEL)
    x_s = x_sample.transpose(1, 0, 2).reshape(t_s * n_s, D_MODEL)
    tm_mm, tm_s = 1024, t_s * n_s
    tm_seq = 256

    h_p = _prenorm(gp_, x_p, g_pre_mix, 0, SCALE_M, SHIFT_M, 512)
    h_s = _prenorm(gs_, x_s, g_pre_mix, 0, SCALE_M, SHIFT_M, tm_s)

    outs = {k: [] for k in ("conv_p", "xb_s", "pool_p", "pin_s", "ffn_p", "gp_s", "v_s")}
    pin_lo = COL_PIN * 1024
    for l in range(DEPTH):
        nxt = min(l + 1, DEPTH - 1)

        proj_p = _mm(h_p, w_in, l, tm_mm, 1024, "proj_prompt")
        proj_s = _mm(h_s, w_in, l, tm_s, 1024, "proj_sample")
        acts_p, nconv_p = _seq_p(gp_, proj_p, sp, l, tm_seq)
        acts_s, vn_s, xb_s = _seq_s(gs_, proj_s, sconv_t, spool_t, sp, l)
        mix_p = _mixmm(acts_p, proj_p, sp, l, tm_seq)
        mix_s = _mixmm(acts_s.reshape(gs_.rows, 3 * 1024), proj_s, sp, l, tm_seq)
        x_p, h_p = _g1(gp_, mix_p, w_o, x_p, l, GATE_M, g_post_mix, l, SCALE_F, SHIFT_F,
                       g_pre_ffn, tm_seq, "wo_prompt")
        x_s, h_s = _g1(gs_, mix_s, w_o, x_s, l, GATE_M, g_post_mix, l, SCALE_F, SHIFT_F,
                       g_pre_ffn, tm_s, "wo_sample")

        upv_p = _mm(h_p, w_up, l, tm_mm, 1024, "up_prompt")
        upv_s = _mm(h_s, w_up, l, tm_s, 1024, "up_sample")
        act_p = _ffn_act_p(gp_, upv_p, sp, l, tm_seq)
        act_s = _ffn_act_s(gs_, upv_s, sffn_t, sp, l)
        x_p, h_p = _g1(gp_, act_p, w_down_bf, x_p, l, GATE_F, g_post_ffn, nxt, SCALE_M, SHIFT_M,
                       g_pre_mix, tm_seq, "down_prompt")
        x_s, h_s = _g1(gs_, act_s, w_down_bf, x_s, l, GATE_F, g_post_ffn, nxt, SCALE_M, SHIFT_M,
                       g_pre_mix, tm_s, "down_sample")

        outs["conv_p"].append(nconv_p)
        outs["xb_s"].append(xb_s)
        outs["pool_p"].append(
            proj_p.reshape(n_p, t_p, N_IN)[:, t_p - POOL_BUF:, pin_lo:pin_lo + D_C])
        outs["pin_s"].append(proj_s.reshape(t_s, n_s, N_IN)[:, :, pin_lo:pin_lo + D_C])
        outs["ffn_p"].append(upv_p.reshape(n_p, t_p, 2 * D_FF)[:, t_p - (K_FFN - 1):, :D_FF])
        outs["gp_s"].append(upv_s.reshape(t_s, n_s, 2 * D_FF)[t_s - (K_FFN - 1):, :, :D_FF])
        outs["v_s"].append(vn_s)

    def seq_major(parts):
        return jnp.stack(parts).astype(F32).transpose(0, 2, 1, 3)

    y_p = x_p.reshape(n_p, t_p, D_MODEL)
    y_s = x_s.reshape(t_s, n_s, D_MODEL).transpose(1, 0, 2)
    new_conv_s = jnp.concatenate([state_conv[:, :, t_s:], seq_major(outs["xb_s"])], axis=2)
    new_pool_s = jnp.concatenate([state_pool[:, :, t_s:], seq_major(outs["pin_s"])], axis=2)
    return (y_p, y_s, jnp.stack(outs["conv_p"]), new_conv_s,
            jnp.stack(outs["pool_p"]).astype(F32), new_pool_s,
            jnp.stack(outs["ffn_p"]).astype(F32), seq_major(outs["gp_s"]),
            seq_major(outs["v_s"]))
```

```python
import functools

import jax
import jax.numpy as jnp
from jax import lax
from jax.experimental import pallas as pl
from jax.experimental.pallas import tpu as pltpu

F32 = jnp.float32
BF16 = jnp.bfloat16

D_MODEL = 2048
DEPTH = 4
PAST_LEN = 16384
CHUNK = 128
D_A = D_MODEL // 2
D_B = D_MODEL // 2
D_C = D_MODEL // 2
G_A = D_A // CHUNK
K_CONV = 31
POOL_WINDOWS = (2, 4, 8, 16)
N_POOL = len(POOL_WINDOWS)
D_CG = D_C // N_POOL
D_CG_OUT = D_MODEL // N_POOL
POOL_BUF = max(POOL_WINDOWS) - 1
K_FFN = 3
D_FF = 5504
N_IN = 2 * D_A + 2 * D_B + D_C + 3 * D_MODEL
EPS = 1e-6

LANES = 128
SUBLANES = 8
ROW_VREGS = D_B // LANES

COL_U, COL_V, COL_GA, COL_GB, COL_PIN, COL_GATES = 0, 1, 2, 3, 4, 5

CONV_HALO = 32
POOL_HALO = 16
SEQ_TB = 8
ROW_CHUNK = 128
FF_TILE = 512
FF_SPLIT = D_FF % FF_TILE

VMEM_LIMIT = 56 * 1024 * 1024


def _params(sem):
    return pltpu.CompilerParams(dimension_semantics=sem, vmem_limit_bytes=VMEM_LIMIT)


def _gelu(x):
    return jax.nn.gelu(x, approximate=True)


def _rmsnorm(x, g):
    return x * lax.rsqrt(jnp.mean(x * x, axis=-1, keepdims=True) + EPS) * g


def _layernorm(x, g, b):
    mu = jnp.mean(x, axis=-1, keepdims=True)
    xc = x - mu
    var = jnp.mean(xc * xc, axis=-1, keepdims=True)
    return xc * lax.rsqrt(var + EPS) * g + b


def _rows(m, tm):
    rm, c = m.shape
    if rm == 1 or rm == tm:
        return m
    return jnp.broadcast_to(m[None], (tm // rm, rm, c)).reshape(tm, c)


def _resident(shape, index_map):
    return pl.BlockSpec(shape, index_map, pipeline_mode=pl.Buffered(1))


def _mods_kernel(c_ref, w_ref, b_ref, o_ref):
    c = c_ref[...]
    a = (c * jax.nn.sigmoid(c)).astype(BF16)
    o_ref[...] = jnp.dot(a, w_ref[...].astype(BF16), preferred_element_type=F32) + b_ref[...]


def _mods(c_all, ada_w, ada_b, tn=1024):
    rows = c_all.shape[0]
    n_out = ada_w.shape[-1]
    return pl.pallas_call(
        _mods_kernel,
        grid=(DEPTH, n_out // tn),
        in_specs=[
            pl.BlockSpec((rows, D_MODEL), lambda l, j: (0, 0)),
            pl.BlockSpec((None, D_MODEL, tn), lambda l, j: (l, 0, j)),
            pl.BlockSpec((None, 1, tn), lambda l, j: (l, 0, j)),
        ],
        out_specs=pl.BlockSpec((None, rows, tn), lambda l, j: (l, 0, j)),
        out_shape=jax.ShapeDtypeStruct((DEPTH, rows, n_out), F32),
        compiler_params=_params(("arbitrary", "arbitrary")),
        name="mods",
    )(c_all, ada_w, ada_b.reshape(DEPTH, 1, n_out))


class _Group:
    def __init__(self, n_seq, t_len, time_major, mods):
        self.n_seq = n_seq
        self.t_len = t_len
        self.time_major = time_major
        self.rows = n_seq * t_len
        self.mods = mods

    def mod_spec(self, tm, layer, chunk):
        if self.time_major:
            return pl.BlockSpec((None, self.n_seq, D_MODEL), lambda i: (layer, 0, chunk))
        per_seq = self.t_len // tm
        return pl.BlockSpec((None, None, 1, D_MODEL), lambda i: (layer, i // per_seq, 0, chunk))


def _vec_spec(width, layer):
    return pl.BlockSpec((None, 1, width), lambda *_: (layer, 0, 0))


def _vec(p):
    return p.reshape(DEPTH, 1, p.shape[-1])


def _prenorm_kernel(x_ref, g_ref, sc_ref, sh_ref, h_ref):
    x = x_ref[...]
    tm = x.shape[0]
    y = _rmsnorm(x, g_ref[...])
    h_ref[...] = (y * (1.0 + _rows(sc_ref[...], tm)) + _rows(sh_ref[...], tm)).astype(BF16)


def _prenorm(grp, x, g, layer, c_scale, c_shift, tm):
    return pl.pallas_call(
        _prenorm_kernel,
        grid=(grp.rows // tm,),
        in_specs=[
            pl.BlockSpec((tm, D_MODEL), lambda i: (i, 0)),
            _vec_spec(D_MODEL, layer),
            grp.mod_spec(tm, layer, c_scale),
            grp.mod_spec(tm, layer, c_shift),
        ],
        out_specs=pl.BlockSpec((tm, D_MODEL), lambda i: (i, 0)),
        out_shape=jax.ShapeDtypeStruct((grp.rows, D_MODEL), BF16),
        compiler_params=_params(("arbitrary",)),
        name="prenorm",
    )(x, _vec(g), grp.mods, grp.mods)


def _mm_kernel(a_ref, w_ref, o_ref, wb):
    @pl.when(pl.program_id(1) == 0)
    def _():
        wb[...] = w_ref[...].astype(BF16)

    o_ref[...] = jnp.dot(a_ref[...], wb[...], preferred_element_type=F32).astype(o_ref.dtype)


def _mm(a, w, layer, tm, tn, name):
    m, k = a.shape
    n = w.shape[-1]
    return pl.pallas_call(
        _mm_kernel,
        grid=(pl.cdiv(n, tn), m // tm),
        in_specs=[
            pl.BlockSpec((tm, k), lambda j, i: (i, 0)),
            pl.BlockSpec((None, k, tn), lambda j, i: (layer, 0, j)),
        ],
        out_specs=pl.BlockSpec((tm, tn), lambda j, i: (i, j)),
        out_shape=jax.ShapeDtypeStruct((m, n), BF16),
        scratch_shapes=[pltpu.VMEM((k, tn), BF16)],
        compiler_params=_params(("arbitrary", "arbitrary")),
        name=name,
    )(a, w)


def _to_rowvreg(ref, row0, x):
    tm = x.shape[0]
    for c in range(ROW_VREGS):
        ref[pl.ds(row0 * ROW_VREGS + c, tm, stride=ROW_VREGS), :] = x[:, c * LANES:(c + 1) * LANES]


def _from_rowvreg(ref, tm):
    return jnp.concatenate(
        [ref[pl.ds(c, tm, stride=ROW_VREGS), :] for c in range(ROW_VREGS)], axis=-1)


def _seq_p_kernel(u_ref, v_ref, ga_ref, gb_ref, pin_ref, lnvg_ref, lnvb_ref, ws_ref, bmix_ref,
                  wconv_ref, bconv_ref, lncg_ref, lncb_ref,
                  acts_ref, nconv_ref, xrv, yrv, prv, qrv, *, tm, n_tiles, start):
    i = pl.program_id(1)
    rv = ROW_VREGS

    vn = _layernorm(_gelu(v_ref[...].astype(F32)), lnvg_ref[...], lnvb_ref[...]).astype(BF16)
    gu = _gelu(u_ref[...].astype(F32))
    row = lax.broadcasted_iota(jnp.int32, (CHUNK, CHUNK), 0)
    col = lax.broadcasted_iota(jnp.int32, (CHUNK, CHUNK), 1)
    for g in range(G_A):
        lanes = slice(g * CHUNK, (g + 1) * CHUNK)
        wg = jnp.where(row >= col, ws_ref[g], 0.0).astype(BF16)
        for c in range(tm // CHUNK):
            rws = slice(c * CHUNK, (c + 1) * CHUNK)
            mixed = jnp.dot(wg, vn[rws, lanes], preferred_element_type=F32) + bmix_ref[:, lanes]
            acts_ref[rws, lanes] = (gu[rws, lanes] * mixed).astype(BF16)

    @pl.when(i == 0)
    def _():
        xrv[0:CONV_HALO * rv, :] = jnp.zeros((CONV_HALO * rv, LANES), F32)
        prv[0:POOL_HALO * rv, :] = jnp.zeros((POOL_HALO * rv, LANES), F32)

    @pl.when(i > 0)
    def _():
        xrv[0:CONV_HALO * rv, :] = xrv[tm * rv:(tm + CONV_HALO) * rv, :]
        prv[0:POOL_HALO * rv, :] = prv[tm * rv:(tm + POOL_HALO) * rv, :]

    xb = ga_ref[...].astype(F32) * jax.nn.sigmoid(gb_ref[...].astype(F32))
    _to_rowvreg(xrv, CONV_HALO, xb)

    @pl.when(i == n_tiles - 1)
    def _():
        nconv_ref[...] = xb[tm - (K_CONV - 1):tm, :]

    bias = bconv_ref[...]
    first = CONV_HALO - (K_CONV - 1)

    def conv_body(it, carry):
        t0 = it * SEQ_TB
        base = pl.multiple_of((t0 + first) * rv, rv)
        taps = [wconv_ref[k * rv:(k + 1) * rv, :] for k in range(K_CONV)]
        acc = [bias] * SEQ_TB
        for j in range(SEQ_TB + K_CONV - 1):
            xj = xrv[pl.ds(base + j * rv, rv), :]
            for tt in range(SEQ_TB):
                k = j - tt
                if 0 <= k < K_CONV:
                    acc[tt] = acc[tt] + taps[k] * xj
        for tt in range(SEQ_TB):
            yrv[pl.ds(pl.multiple_of((t0 + tt) * rv, rv), rv), :] = acc[tt]
        return carry

    lax.fori_loop(0, tm // SEQ_TB, conv_body, 0)
    zb = _layernorm(_from_rowvreg(yrv, tm), lncg_ref[...], lncb_ref[...])
    acts_ref[:, D_A:D_A + D_B] = (zb * jax.nn.sigmoid(zb)).astype(BF16)

    _to_rowvreg(prv, POOL_HALO, pin_ref[...].astype(F32))
    grp_of_sublane = lax.broadcasted_iota(jnp.int32, (rv, LANES), 0) // (rv // N_POOL)
    window = jnp.left_shift(POOL_WINDOWS[0], grp_of_sublane)
    pos0 = start + i * tm

    def pool_body(it, carry):
        t0 = it * SEQ_TB
        base = pl.multiple_of((t0 + POOL_HALO - POOL_BUF) * rv, rv)
        x = [prv[pl.ds(base + j * rv, rv), :] for j in range(SEQ_TB + POOL_BUF)]
        for tt in range(SEQ_TB):
            cur = POOL_BUF + tt
            s = x[cur] + x[cur - 1]
            lo = 2
            for gi in range(1, N_POOL):
                hi = POOL_WINDOWS[gi]
                part = x[cur - lo]
                for j in range(lo + 1, hi):
                    part = part + x[cur - j]
                s = s + jnp.where(grp_of_sublane >= gi, part, 0.0)
                lo = hi
            cnt = jnp.minimum(pos0 + t0 + tt + 1, window).astype(F32)
            qrv[pl.ds(pl.multiple_of((t0 + tt) * rv, rv), rv), :] = s / cnt - x[cur]
        return carry

    lax.fori_loop(0, tm // SEQ_TB, pool_body, 0)
    acts_ref[:, D_A + D_B:D_A + D_B + D_C] = _from_rowvreg(qrv, tm).astype(BF16)


def _seq_p(grp, proj, p, layer, tm):
    n_tiles = grp.t_len // tm
    rv = ROW_VREGS

    def col(c):
        return pl.BlockSpec((tm, 1024), lambda n, i: (n * n_tiles + i, c))

    def lay(*shape):
        return pl.BlockSpec((None,) + shape, lambda n, i: (layer,) + (0,) * len(shape))

    return pl.pallas_call(
        functools.partial(_seq_p_kernel, tm=tm, n_tiles=n_tiles, start=0),
        grid=(grp.n_seq, n_tiles),
        in_specs=[col(COL_U), col(COL_V), col(COL_GA), col(COL_GB), col(COL_PIN),
                  lay(1, D_A), lay(1, D_A), lay(G_A, CHUNK, CHUNK), lay(CHUNK, D_A),
                  lay(K_CONV * rv, LANES), lay(rv, LANES), lay(1, D_B), lay(1, D_B)],
        out_specs=[pl.BlockSpec((tm, 3 * 1024), lambda n, i: (n * n_tiles + i, 0)),
                   pl.BlockSpec((None, K_CONV - 1, D_B), lambda n, i: (n, 0, 0))],
        out_shape=[jax.ShapeDtypeStruct((grp.rows, 3 * 1024), BF16),
                   jax.ShapeDtypeStruct((grp.n_seq, K_CONV - 1, D_B), F32)],
        scratch_shapes=[pltpu.VMEM(((CONV_HALO + tm) * rv, LANES), F32),
                        pltpu.VMEM((tm * rv, LANES), F32),
                        pltpu.VMEM(((POOL_HALO + tm) * rv, LANES), F32),
                        pltpu.VMEM((tm * rv, LANES), F32)],
        compiler_params=_params(("arbitrary", "arbitrary")),
        name="seq_prompt",
    )(proj, proj, proj, proj, proj, p["ln_v_g"], p["ln_v_b"], p["w_spatial"], p["b_mix"],
      p["w_dwconv_rv"], p["b_dwconv_rv"], p["ln_conv_g"], p["ln_conv_b"])


def _seq_s_kernel(u_ref, v_ref, ga_ref, gb_ref, pin_ref, sconv_ref, spool_ref,
                  lnvg_ref, lnvb_ref, wv_ref, bv_ref, wconv_ref, bconv_ref, lncg_ref, lncb_ref,
                  acts_ref, vn_ref, xb_ref, *, t_len, nb, start):
    vn = []
    for t in range(t_len):
        vn_t = _layernorm(_gelu(v_ref[t].astype(F32)), lnvg_ref[...], lnvb_ref[...])
        vn_ref[t] = vn_t
        vn.append(vn_t)
        mixed = jnp.broadcast_to(bv_ref[t:t + 1, :], (nb, D_A))
        for s in range(t + 1):
            mixed = mixed + wv_ref[t * t_len + s:t * t_len + s + 1, :] * vn[s]
        acts_ref[t, :, 0:D_A] = (_gelu(u_ref[t].astype(F32)) * mixed).astype(BF16)

    hist = K_CONV - 1
    acc = [jnp.broadcast_to(bconv_ref[...], (nb, D_B)) for _ in range(t_len)]
    for j in range(hist + t_len):
        if j < hist:
            xj = sconv_ref[j]
        else:
            xj = ga_ref[j - hist].astype(F32) * jax.nn.sigmoid(gb_ref[j - hist].astype(F32))
            xb_ref[j - hist] = xj
        for t in range(t_len):
            k = j - t
            if 0 <= k < K_CONV:
                acc[t] = acc[t] + wconv_ref[k:k + 1, :] * xj
    for t in range(t_len):
        zb = _layernorm(acc[t], lncg_ref[...], lncb_ref[...])
        acts_ref[t, :, D_A:D_A + D_B] = (zb * jax.nn.sigmoid(zb)).astype(BF16)

    rows = [spool_ref[j] for j in range(POOL_BUF)]
    rows += [pin_ref[t].astype(F32) for t in range(t_len)]
    for t in range(t_len):
        for gi, w in enumerate(POOL_WINDOWS):
            lanes = slice(gi * D_CG, (gi + 1) * D_CG)
            s = rows[POOL_BUF + t][:, lanes]
            for j in range(1, w):
                s = s + rows[POOL_BUF + t - j][:, lanes]
            cnt = float(min(start + t + 1, w))
            acts_ref[t, :, D_A + D_B + gi * D_CG:D_A + D_B + (gi + 1) * D_CG] = (
                s / cnt - rows[POOL_BUF + t][:, lanes]).astype(BF16)


def _seq_s(grp, proj, sconv_t, spool_t, p, layer, nb=32):
    t_len, n_seq = grp.t_len, grp.n_seq
    proj3 = proj.reshape(t_len, n_seq, N_IN)
    hist = K_CONV - 1

    def col(c):
        return pl.BlockSpec((t_len, nb, 1024), lambda b: (0, b, c))

    def lay(*shape):
        return pl.BlockSpec((None,) + shape, lambda b: (layer,) + (0,) * len(shape))

    return pl.pallas_call(
        functools.partial(_seq_s_kernel, t_len=t_len, nb=nb, start=PAST_LEN),
        grid=(n_seq // nb,),
        in_specs=[col(COL_U), col(COL_V), col(COL_GA), col(COL_GB), col(COL_PIN),
                  pl.BlockSpec((None, hist, nb, D_B), lambda b: (layer, 0, b, 0)),
                  pl.BlockSpec((None, POOL_BUF, nb, D_C), lambda b: (layer, 0, b, 0)),
                  lay(1, D_A), lay(1, D_A), lay(t_len * t_len, D_A), lay(t_len, D_A),
                  lay(K_CONV, D_B), lay(1, D_B), lay(1, D_B), lay(1, D_B)],
        out_specs=[pl.BlockSpec((t_len, nb, 3 * 1024), lambda b: (0, b, 0)),
                   pl.BlockSpec((t_len, nb, D_A), lambda b: (0, b, 0)),
                   pl.BlockSpec((t_len, nb, D_B), lambda b: (0, b, 0))],
        out_shape=[jax.ShapeDtypeStruct((t_len, n_seq, 3 * 1024), BF16),
                   jax.ShapeDtypeStruct((t_len, n_seq, D_A), F32),
                   jax.ShapeDtypeStruct((t_len, n_seq, D_B), F32)],
        compiler_params=_params(("arbitrary",)),
        name="seq_sample",
    )(proj3, proj3, proj3, proj3, proj3, sconv_t, spool_t,
      p["ln_v_g"], p["ln_v_b"], p["w_spatial_vec"], p["b_spatial_vec"],
      p["w_dwconv"], p["b_dwconv"], p["ln_conv_g"], p["ln_conv_b"])


def _mixmm_kernel(acts_ref, g0, g1, g2, g3, g4, g5, bgate_ref, wa_ref, wb_ref, wp_ref, ps_ref,
                  o_ref, wa_s, wb_s, wp_s):
    @pl.when(pl.program_id(0) == 0)
    def _():
        wa_s[...] = wa_ref[...].astype(BF16)
        wb_s[...] = wb_ref[...].astype(BF16)
        wp_s[...] = wp_ref[...].astype(BF16)

    gate_refs = (g0, g1, g2, g3, g4, g5)

    def gate(branch, half):
        idx = 2 * branch + half
        pre = gate_refs[idx][...].astype(F32) + bgate_ref[:, idx * 1024:(idx + 1) * 1024]
        return jax.nn.sigmoid(pre)

    ya = jnp.dot(acts_ref[:, 0:D_A], wa_s[...], preferred_element_type=F32)
    yb = jnp.dot(acts_ref[:, D_A:D_A + D_B], wb_s[...], preferred_element_type=F32)
    for half in range(2):
        cols = slice(half * 1024, (half + 1) * 1024)
        yc = jnp.concatenate(
            [jnp.dot(acts_ref[:, D_A + D_B + g * D_CG:D_A + D_B + (g + 1) * D_CG], wp_s[g],
                     preferred_element_type=F32) for g in (2 * half, 2 * half + 1)],
            axis=-1) * ps_ref[:, cols]
        mix = gate(0, half) * ya[:, cols] + gate(1, half) * yb[:, cols] + gate(2, half) * yc
        o_ref[:, cols] = mix.astype(BF16)


def _mixmm(acts, proj, p, layer, tm):
    m = acts.shape[0]

    def gcol(c):
        return pl.BlockSpec((tm, 1024), lambda i: (i, COL_GATES + c))

    def lay(*shape):
        return _resident((None,) + shape, lambda i: (layer,) + (0,) * len(shape))

    return pl.pallas_call(
        _mixmm_kernel,
        grid=(m // tm,),
        in_specs=[pl.BlockSpec((tm, 3 * 1024), lambda i: (i, 0))] + [gcol(c) for c in range(6)] + [
            lay(1, 3 * D_MODEL), lay(D_A, D_MODEL), lay(D_B, D_MODEL),
            lay(N_POOL, D_CG, D_CG_OUT), lay(1, D_MODEL)],
        out_specs=pl.BlockSpec((tm, D_MODEL), lambda i: (i, 0)),
        out_shape=jax.ShapeDtypeStruct((m, D_MODEL), BF16),
        scratch_shapes=[pltpu.VMEM((D_A, D_MODEL), BF16), pltpu.VMEM((D_B, D_MODEL), BF16),
                        pltpu.VMEM((N_POOL, D_CG, D_CG_OUT), BF16)],
        compiler_params=_params(("arbitrary",)),
        name="mixmm",
    )(acts, proj, proj, proj, proj, proj, proj, p["b_gate"], p["w_a_out"], p["w_b_out"],
      p["w_pool_grp"], p["pool_scale"])


def _g1_kernel(a_ref, w_ref, x_ref, gate_ref, gpost_ref, gnext_ref, sc_ref, sh_ref,
               xo_ref, ho_ref, *scratch):
    tm = x_ref.shape[0]
    if scratch:
        wb, = scratch

        @pl.when(pl.program_id(0) == 0)
        def _():
            wb[...] = w_ref[...].astype(BF16)

        w = wb[...]
    else:
        w = w_ref[...]
    y = jnp.dot(a_ref[...], w, preferred_element_type=F32)
    x = x_ref[...] + _rows(gate_ref[...], tm) * _rmsnorm(y, gpost_ref[...])
    xo_ref[...] = x
    h = _rmsnorm(x, gnext_ref[...]) * (1.0 + _rows(sc_ref[...], tm)) + _rows(sh_ref[...], tm)
    ho_ref[...] = h.astype(BF16)


def _g1(grp, a, w, x, layer, c_gate, g_post, layer_next, c_scale, c_shift, g_next, tm, name):
    k = a.shape[1]
    scratch = [pltpu.VMEM((k, D_MODEL), BF16)] if w.dtype == F32 else []
    return pl.pallas_call(
        _g1_kernel,
        grid=(grp.rows // tm,),
        in_specs=[
            pl.BlockSpec((tm, k), lambda i: (i, 0)),
            _resident((None, k, D_MODEL), lambda i: (layer, 0, 0)),
            pl.BlockSpec((tm, D_MODEL), lambda i: (i, 0)),
            grp.mod_spec(tm, layer, c_gate),
            _vec_spec(D_MODEL, layer),
            _vec_spec(D_MODEL, layer_next),
            grp.mod_spec(tm, layer_next, c_scale),
            grp.mod_spec(tm, layer_next, c_shift),
        ],
        out_specs=[pl.BlockSpec((tm, D_MODEL), lambda i: (i, 0)),
                   pl.BlockSpec((tm, D_MODEL), lambda i: (i, 0))],
        out_shape=[jax.ShapeDtypeStruct((grp.rows, D_MODEL), F32),
                   jax.ShapeDtypeStruct((grp.rows, D_MODEL), BF16)],
        scratch_shapes=scratch,
        compiler_params=_params(("arbitrary",)),
        name=name,
    )(a, w, x, grp.mods, _vec(g_post), _vec(g_next), grp.mods, grp.mods)


def _ffn_up_p_kernel(h_ref, wg_ref, wv0_ref, wv1_ref, cw_ref, cb_ref, act_ref, new_ref,
                     wb, carry, *, tm, tiles_per_seq):
    i = pl.program_id(1)
    seq_tile = i % tiles_per_seq

    @pl.when(i == 0)
    def _():
        wb[:, 0:FF_TILE] = wg_ref[...].astype(BF16)
        wb[:, FF_TILE:2 * FF_TILE - FF_SPLIT] = wv0_ref[:, FF_SPLIT:].astype(BF16)
        wb[:, 2 * FF_TILE - FF_SPLIT:] = wv1_ref[:, :FF_SPLIT].astype(BF16)

    @pl.when(seq_tile == 0)
    def _():
        carry[...] = jnp.zeros(carry.shape, F32)

    rc = ROW_CHUNK
    row = lax.broadcasted_iota(jnp.int32, (rc, 1), 0)
    w0, w1, w2, b = cw_ref[0:1, :], cw_ref[1:2, :], cw_ref[2:3, :], cb_ref[...]
    h1, h2 = carry[SUBLANES - 1:SUBLANES, :], carry[SUBLANES - 2:SUBLANES - 1, :]
    gp = None
    for q in range(tm // rc):
        rows = slice(q * rc, (q + 1) * rc)
        acc = jnp.dot(h_ref[rows, :], wb[...], preferred_element_type=F32)
        gp, val = acc[:, :FF_TILE], acc[:, FF_TILE:]
        prev1 = jnp.where(row == 0, h1, pltpu.roll(gp, 1, axis=0))
        prev2 = jnp.where(row == 0, h2, jnp.where(row == 1, h1, pltpu.roll(gp, 2, axis=0)))
        gc = w0 * prev2 + w1 * prev1 + w2 * gp + b
        act_ref[rows, :] = (_gelu(gc) * val).astype(BF16)
        h1, h2 = gp[rc - 1:rc, :], gp[rc - 2:rc - 1, :]
    carry[...] = gp[rc - SUBLANES:rc, :]

    @pl.when(seq_tile == tiles_per_seq - 1)
    def _():
        new_ref[...] = gp[rc - (K_FFN - 1):rc, :]


def _ffn_up_p(grp, h, w_up, p, layer, tm):
    tiles_per_seq = grp.t_len // tm
    n_j = pl.cdiv(D_FF, FF_TILE)
    val0 = D_FF // FF_TILE

    def wblk(off):
        return pl.BlockSpec((None, D_MODEL, FF_TILE), lambda j, i: (layer, 0, j + off))

    return pl.pallas_call(
        functools.partial(_ffn_up_p_kernel, tm=tm, tiles_per_seq=tiles_per_seq),
        grid=(n_j, grp.rows // tm),
        in_specs=[pl.BlockSpec((tm, D_MODEL), lambda j, i: (i, 0)),
                  wblk(0), wblk(val0), wblk(val0 + 1),
                  pl.BlockSpec((None, K_FFN, FF_TILE), lambda j, i: (layer, 0, j)),
                  pl.BlockSpec((None, 1, FF_TILE), lambda j, i: (layer, 0, j))],
        out_specs=[pl.BlockSpec((tm, FF_TILE), lambda j, i: (i, j)),
                   pl.BlockSpec((None, K_FFN - 1, FF_TILE),
                                lambda j, i: (i // tiles_per_seq, 0, j))],
        out_shape=[jax.ShapeDtypeStruct((grp.rows, D_FF), BF16),
                   jax.ShapeDtypeStruct((grp.n_seq, K_FFN - 1, D_FF), F32)],
        scratch_shapes=[pltpu.VMEM((D_MODEL, 2 * FF_TILE), BF16),
                        pltpu.VMEM((SUBLANES, FF_TILE), F32)],
        compiler_params=_params(("arbitrary", "arbitrary")),
        name="ffn_up_prompt",
    )(h, w_up, w_up, w_up, p["w_ffn_conv"], p["b_ffn_conv"])


def _ffn_act_s_kernel(gp_ref, val_ref, st_ref, w_ref, b_ref, o_ref, *, t_len):
    hist = K_FFN - 1
    rows = [st_ref[j] for j in range(hist)]
    rows += [gp_ref[t].astype(F32) for t in range(t_len)]
    for t in range(t_len):
        gc = b_ref[...] + w_ref[0:1, :] * rows[t]
        for k in range(1, K_FFN):
            gc = gc + w_ref[k:k + 1, :] * rows[t + k]
        o_ref[t] = (_gelu(gc) * val_ref[t].astype(F32)).astype(BF16)


def _ffn_act_s(grp, upv, sffn_t, p, layer, nb=32):
    t_len, n_seq = grp.t_len, grp.n_seq
    upv3 = upv.reshape(t_len, n_seq, 2 * D_FF)
    hist = K_FFN - 1

    def lay(*shape):
        return pl.BlockSpec((None,) + shape, lambda b: (layer,) + (0,) * len(shape))

    return pl.pallas_call(
        functools.partial(_ffn_act_s_kernel, t_len=t_len),
        grid=(n_seq // nb,),
        in_specs=[pl.BlockSpec((t_len, nb, D_FF), lambda b: (0, b, 0)),
                  pl.BlockSpec((t_len, nb, D_FF), lambda b: (0, b, 1)),
                  pl.BlockSpec((None, hist, nb, D_FF), lambda b: (layer, 0, b, 0)),
                  lay(K_FFN, D_FF), lay(1, D_FF)],
        out_specs=pl.BlockSpec((t_len, nb, D_FF), lambda b: (0, b, 0)),
        out_shape=jax.ShapeDtypeStruct((t_len, n_seq, D_FF), BF16),
        compiler_params=_params(("arbitrary",)),
        name="ffn_act_sample",
    )(upv3, upv3, sffn_t, p["w_ffn_conv"], p["b_ffn_conv"]).reshape(grp.rows, D_FF)


SHIFT_M, SCALE_M, GATE_M, SHIFT_F, SCALE_F, GATE_F = range(6)


def _small_params(p, t_s):
    ws, bs = p["w_spatial"], p["b_spatial"]
    b_mix = jnp.repeat(bs.transpose(0, 2, 1), CHUNK, axis=2)
    w_vec = jnp.repeat(ws[:, :, :t_s, :t_s].transpose(0, 2, 3, 1), CHUNK, axis=3)
    return dict(
        pool_scale=_vec(p["pool_scale"]), b_gate=_vec(p["b_gate"]),
        ln_v_g=_vec(p["ln_v_g"]), ln_v_b=_vec(p["ln_v_b"]),
        w_spatial=ws, b_mix=b_mix,
        w_spatial_vec=w_vec.reshape(DEPTH, t_s * t_s, D_A), b_spatial_vec=b_mix[:, :t_s],
        w_dwconv=p["w_dwconv"], b_dwconv=_vec(p["b_dwconv"]),
        w_dwconv_rv=p["w_dwconv"].reshape(DEPTH, K_CONV * ROW_VREGS, LANES),
        b_dwconv_rv=p["b_dwconv"].reshape(DEPTH, ROW_VREGS, LANES),
        ln_conv_g=_vec(p["ln_conv_g"]), ln_conv_b=_vec(p["ln_conv_b"]),
        w_ffn_conv=p["w_ffn_conv"], b_ffn_conv=_vec(p["b_ffn_conv"]),
        w_a_out=p["w_a_out"], w_b_out=p["w_b_out"], w_pool_grp=p["w_pool_grp"],
    )


def kernel(x_prompt, x_sample, c_prompt, c_sample, state_conv, state_pool, state_ffn_conv, ada_w, ada_b, g_pre_mix, g_post_mix, g_pre_ffn, g_post_ffn, w_in, b_gate, ln_v_g, ln_v_b, w_spatial, b_spatial, w_a_out, w_dwconv, b_dwconv, ln_conv_g, ln_conv_b, w_b_out, w_pool_grp, pool_scale, w_o, w_up, w_ffn_conv, b_ffn_conv, w_down):
    n_p, t_p, _ = x_prompt.shape
    n_s, t_s, _ = x_sample.shape
    sp = _small_params(dict(
        b_gate=b_gate, ln_v_g=ln_v_g, ln_v_b=ln_v_b, w_spatial=w_spatial, b_spatial=b_spatial,
        w_a_out=w_a_out, w_dwconv=w_dwconv, b_dwconv=b_dwconv, ln_conv_g=ln_conv_g,
        ln_conv_b=ln_conv_b, w_b_out=w_b_out, w_pool_grp=w_pool_grp, pool_scale=pool_scale,
        w_ffn_conv=w_ffn_conv, b_ffn_conv=b_ffn_conv), t_s)
    w_down_bf = w_down.astype(BF16)

    pad = (-(n_p + n_s)) % SUBLANES
    c_all = jnp.concatenate([c_sample, c_prompt, jnp.zeros((pad, D_MODEL), F32)], axis=0)
    mods = _mods(c_all, ada_w, ada_b)
    gs_ = _Group(n_s, t_s, True, mods)
    gp_ = _Group(n_p, t_p, False, mods[:, n_s:n_s + n_p].reshape(DEPTH, n_p, 1, 6 * D_MODEL))

    sconv_t = state_conv.transpose(0, 2, 1, 3)
    spool_t = state_pool.transpose(0, 2, 1, 3)
    sffn_t = state_ffn_conv.transpose(0, 2, 1, 3)

    x_p = x_prompt.reshape(n_p * t_p, D_MODEL)
    x_s = x_sample.transpose(1, 0, 2).reshape(t_s * n_s, D_MODEL)
    tm_mm, tm_s = t_p, t_s * n_s
    tm_seq = 256

    h_p = _prenorm(gp_, x_p, g_pre_mix, 0, SCALE_M, SHIFT_M, 512)
    h_s = _prenorm(gs_, x_s, g_pre_mix, 0, SCALE_M, SHIFT_M, tm_s)

    outs = {k: [] for k in ("conv_p", "xb_s", "pool_p", "pin_s", "ffn_p", "gp_s", "v_s")}
    pin_lo = COL_PIN * 1024
    for l in range(DEPTH):
        nxt = min(l + 1, DEPTH - 1)

        proj_p = _mm(h_p, w_in, l, tm_mm, 1024, "proj_prompt")
        proj_s = _mm(h_s, w_in, l, tm_s, 1024, "proj_sample")
        acts_p, nconv_p = _seq_p(gp_, proj_p, sp, l, tm_seq)
        acts_s, vn_s, xb_s = _seq_s(gs_, proj_s, sconv_t, spool_t, sp, l)
        mix_p = _mixmm(acts_p, proj_p, sp, l, tm_seq)
        mix_s = _mixmm(acts_s.reshape(gs_.rows, 3 * 1024), proj_s, sp, l, tm_seq)
        x_p, h_p = _g1(gp_, mix_p, w_o, x_p, l, GATE_M, g_post_mix, l, SCALE_F, SHIFT_F,
                       g_pre_ffn, tm_seq, "wo_prompt")
        x_s, h_s = _g1(gs_, mix_s, w_o, x_s, l, GATE_M, g_post_mix, l, SCALE_F, SHIFT_F,
                       g_pre_ffn, tm_s, "wo_sample")

        act_p, nffn_p = _ffn_up_p(gp_, h_p, w_up, sp, l, tm_mm)
        upv_s = _mm(h_s, w_up, l, tm_s, 1024, "up_sample")
        act_s = _ffn_act_s(gs_, upv_s, sffn_t, sp, l)
        x_p, h_p = _g1(gp_, act_p, w_down_bf, x_p, l, GATE_F, g_post_ffn, nxt, SCALE_M, SHIFT_M,
                       g_pre_mix, tm_seq, "down_prompt")
        x_s, h_s = _g1(gs_, act_s, w_down_bf, x_s, l, GATE_F, g_post_ffn, nxt, SCALE_M, SHIFT_M,
                       g_pre_mix, tm_s, "down_sample")

        outs["conv_p"].append(nconv_p)
        outs["xb_s"].append(xb_s)
        outs["pool_p"].append(
            proj_p.reshape(n_p, t_p, N_IN)[:, t_p - POOL_BUF:, pin_lo:pin_lo + D_C])
        outs["pin_s"].append(proj_s.reshape(t_s, n_s, N_IN)[:, :, pin_lo:pin_lo + D_C])
        outs["ffn_p"].append(nffn_p)
        outs["gp_s"].append(upv_s.reshape(t_s, n_s, 2 * D_FF)[t_s - (K_FFN - 1):, :, :D_FF])
        outs["v_s"].append(vn_s)

    def seq_major(parts):
        return jnp.stack(parts).astype(F32).transpose(0, 2, 1, 3)

    y_p = x_p.reshape(n_p, t_p, D_MODEL)
    y_s = x_s.reshape(t_s, n_s, D_MODEL).transpose(1, 0, 2)
    new_conv_s = jnp.concatenate([state_conv[:, :, t_s:], seq_major(outs["xb_s"])], axis=2)
    new_pool_s = jnp.concatenate([state_pool[:, :, t_s:], seq_major(outs["pin_s"])], axis=2)
    return (y_p, y_s, jnp.stack(outs["conv_p"]), new_conv_s,
            jnp.stack(outs["pool_p"]).astype(F32), new_pool_s,
            jnp.stack(outs["ffn_p"]).astype(F32), seq_major(outs["gp_s"]),
            seq_major(outs["v_s"]))
```

```python
import functools

import jax
import jax.numpy as jnp
from jax import lax
from jax.experimental import pallas as pl
from jax.experimental.pallas import tpu as pltpu

F32 = jnp.float32
BF16 = jnp.bfloat16

D_MODEL = 2048
DEPTH = 4
PAST_LEN = 16384
CHUNK = 128
D_A = D_MODEL // 2
D_B = D_MODEL // 2
D_C = D_MODEL // 2
G_A = D_A // CHUNK
K_CONV = 31
POOL_WINDOWS = (2, 4, 8, 16)
N_POOL = len(POOL_WINDOWS)
D_CG = D_C // N_POOL
D_CG_OUT = D_MODEL // N_POOL
POOL_BUF = max(POOL_WINDOWS) - 1
K_FFN = 3
D_FF = 5504
N_IN = 2 * D_A + 2 * D_B + D_C + 3 * D_MODEL
EPS = 1e-6

LANES = 128
SUBLANES = 8
ROW_VREGS = D_B // LANES

COL_U, COL_V, COL_GA, COL_GB, COL_PIN, COL_GATES = 0, 1, 2, 3, 4, 5

CONV_HALO = 32
POOL_HALO = 16
SEQ_TB = 16
CONV_PASS_TAPS = 16
ROW_CHUNK = 512
FF_TILE = 512
FF_SPLIT = D_FF % FF_TILE

VMEM_LIMIT = 56 * 1024 * 1024


def _params(sem):
    return pltpu.CompilerParams(dimension_semantics=sem, vmem_limit_bytes=VMEM_LIMIT)


def _gelu(x):
    return jax.nn.gelu(x, approximate=True)


def _rmsnorm(x, g):
    return x * lax.rsqrt(jnp.mean(x * x, axis=-1, keepdims=True) + EPS) * g


def _layernorm(x, g, b):
    mu = jnp.mean(x, axis=-1, keepdims=True)
    xc = x - mu
    var = jnp.mean(xc * xc, axis=-1, keepdims=True)
    return xc * lax.rsqrt(var + EPS) * g + b


def _rows(m, tm):
    rm, c = m.shape
    if rm == 1 or rm == tm:
        return m
    return jnp.broadcast_to(m[None], (tm // rm, rm, c)).reshape(tm, c)


def _resident(shape, index_map):
    return pl.BlockSpec(shape, index_map, pipeline_mode=pl.Buffered(1))


def _mods_kernel(c_ref, w_ref, b_ref, o_ref):
    c = c_ref[...]
    a = (c * jax.nn.sigmoid(c)).astype(BF16)
    o_ref[...] = jnp.dot(a, w_ref[...].astype(BF16), preferred_element_type=F32) + b_ref[...]


def _mods(c_all, ada_w, ada_b, tn=1024):
    rows = c_all.shape[0]
    n_out = ada_w.shape[-1]
    return pl.pallas_call(
        _mods_kernel,
        grid=(DEPTH, n_out // tn),
        in_specs=[
            pl.BlockSpec((rows, D_MODEL), lambda l, j: (0, 0)),
            pl.BlockSpec((None, D_MODEL, tn), lambda l, j: (l, 0, j)),
            pl.BlockSpec((None, 1, tn), lambda l, j: (l, 0, j)),
        ],
        out_specs=pl.BlockSpec((None, rows, tn), lambda l, j: (l, 0, j)),
        out_shape=jax.ShapeDtypeStruct((DEPTH, rows, n_out), F32),
        compiler_params=_params(("arbitrary", "arbitrary")),
        name="mods",
    )(c_all, ada_w, ada_b.reshape(DEPTH, 1, n_out))


class _Group:
    def __init__(self, n_seq, t_len, time_major, mods):
        self.n_seq = n_seq
        self.t_len = t_len
        self.time_major = time_major
        self.rows = n_seq * t_len
        self.mods = mods

    def mod_spec(self, tm, layer, chunk):
        if self.time_major:
            return pl.BlockSpec((None, self.n_seq, D_MODEL), lambda i: (layer, 0, chunk))
        per_seq = self.t_len // tm
        return pl.BlockSpec((None, None, 1, D_MODEL), lambda i: (layer, i // per_seq, 0, chunk))


def _vec_spec(width, layer):
    return pl.BlockSpec((None, 1, width), lambda *_: (layer, 0, 0))


def _vec(p):
    return p.reshape(DEPTH, 1, p.shape[-1])


def _prenorm_kernel(x_ref, g_ref, sc_ref, sh_ref, h_ref):
    x = x_ref[...]
    tm = x.shape[0]
    y = _rmsnorm(x, g_ref[...])
    h_ref[...] = (y * (1.0 + _rows(sc_ref[...], tm)) + _rows(sh_ref[...], tm)).astype(BF16)


def _prenorm(grp, x, g, layer, c_scale, c_shift, tm):
    return pl.pallas_call(
        _prenorm_kernel,
        grid=(grp.rows // tm,),
        in_specs=[
            pl.BlockSpec((tm, D_MODEL), lambda i: (i, 0)),
            _vec_spec(D_MODEL, layer),
            grp.mod_spec(tm, layer, c_scale),
            grp.mod_spec(tm, layer, c_shift),
        ],
        out_specs=pl.BlockSpec((tm, D_MODEL), lambda i: (i, 0)),
        out_shape=jax.ShapeDtypeStruct((grp.rows, D_MODEL), BF16),
        compiler_params=_params(("arbitrary",)),
        name="prenorm",
    )(x, _vec(g), grp.mods, grp.mods)


def _mm_kernel(a_ref, w_ref, o_ref, wb):
    @pl.when(pl.program_id(1) == 0)
    def _():
        wb[...] = w_ref[...].astype(BF16)

    o_ref[...] = jnp.dot(a_ref[...], wb[...], preferred_element_type=F32).astype(o_ref.dtype)


def _mm(a, w, layer, tm, tn, name):
    m, k = a.shape
    n = w.shape[-1]
    return pl.pallas_call(
        _mm_kernel,
        grid=(pl.cdiv(n, tn), m // tm),
        in_specs=[
            pl.BlockSpec((tm, k), lambda j, i: (i, 0)),
            pl.BlockSpec((None, k, tn), lambda j, i: (layer, 0, j)),
        ],
        out_specs=pl.BlockSpec((tm, tn), lambda j, i: (i, j)),
        out_shape=jax.ShapeDtypeStruct((m, n), BF16),
        scratch_shapes=[pltpu.VMEM((k, tn), BF16)],
        compiler_params=_params(("arbitrary", "arbitrary")),
        name=name,
    )(a, w)


def _to_rowvreg(ref, row0, x):
    tm = x.shape[0]
    for c in range(ROW_VREGS):
        ref[pl.ds(row0 * ROW_VREGS + c, tm, stride=ROW_VREGS), :] = x[:, c * LANES:(c + 1) * LANES]


def _from_rowvreg(ref, tm):
    return jnp.concatenate(
        [ref[pl.ds(c, tm, stride=ROW_VREGS), :] for c in range(ROW_VREGS)], axis=-1)


def _seq_p_kernel(u_ref, v_ref, ga_ref, gb_ref, pin_ref, lnvg_ref, lnvb_ref, ws_ref, bmix_ref,
                  wconv_ref, bconv_ref, lncg_ref, lncb_ref,
                  acts_ref, nconv_ref, xrv, yrv, prv, qrv, *, tm, n_tiles, start):
    i = pl.program_id(1)
    rv = ROW_VREGS

    vn = _layernorm(_gelu(v_ref[...].astype(F32)), lnvg_ref[...], lnvb_ref[...]).astype(BF16)
    gu = _gelu(u_ref[...].astype(F32))
    row = lax.broadcasted_iota(jnp.int32, (CHUNK, CHUNK), 0)
    col = lax.broadcasted_iota(jnp.int32, (CHUNK, CHUNK), 1)
    for g in range(G_A):
        lanes = slice(g * CHUNK, (g + 1) * CHUNK)
        wg = jnp.where(row >= col, ws_ref[g], 0.0).astype(BF16)
        for c in range(tm // CHUNK):
            rws = slice(c * CHUNK, (c + 1) * CHUNK)
            mixed = jnp.dot(wg, vn[rws, lanes], preferred_element_type=F32) + bmix_ref[:, lanes]
            acts_ref[rws, lanes] = (gu[rws, lanes] * mixed).astype(BF16)

    @pl.when(i == 0)
    def _():
        xrv[0:CONV_HALO * rv, :] = jnp.zeros((CONV_HALO * rv, LANES), F32)
        prv[0:POOL_HALO * rv, :] = jnp.zeros((POOL_HALO * rv, LANES), F32)

    @pl.when(i > 0)
    def _():
        xrv[0:CONV_HALO * rv, :] = xrv[tm * rv:(tm + CONV_HALO) * rv, :]
        prv[0:POOL_HALO * rv, :] = prv[tm * rv:(tm + POOL_HALO) * rv, :]

    xb = ga_ref[...].astype(F32) * jax.nn.sigmoid(gb_ref[...].astype(F32))
    _to_rowvreg(xrv, CONV_HALO, xb)

    @pl.when(i == n_tiles - 1)
    def _():
        nconv_ref[...] = xb[tm - (K_CONV - 1):tm, :]

    bias = bconv_ref[...]
    first = CONV_HALO - (K_CONV - 1)

    def conv_pass(k_lo, k_hi):
        taps = [wconv_ref[k * rv:(k + 1) * rv, :] for k in range(k_lo, k_hi)]

        def body(it, carry):
            t0 = it * SEQ_TB
            base = pl.multiple_of((t0 + first + k_lo) * rv, rv)
            outs = [pl.ds(pl.multiple_of((t0 + tt) * rv, rv), rv) for tt in range(SEQ_TB)]
            acc = [bias if k_lo == 0 else yrv[outs[tt], :] for tt in range(SEQ_TB)]
            for j in range(SEQ_TB + k_hi - k_lo - 1):
                xj = xrv[pl.ds(base + j * rv, rv), :]
                for tt in range(SEQ_TB):
                    k = j - tt
                    if 0 <= k < k_hi - k_lo:
                        acc[tt] = acc[tt] + taps[k] * xj
            for tt in range(SEQ_TB):
                yrv[outs[tt], :] = acc[tt]
            return carry

        lax.fori_loop(0, tm // SEQ_TB, body, 0)

    for k_lo in range(0, K_CONV, CONV_PASS_TAPS):
        conv_pass(k_lo, min(k_lo + CONV_PASS_TAPS, K_CONV))
    zb = _layernorm(_from_rowvreg(yrv, tm), lncg_ref[...], lncb_ref[...])
    acts_ref[:, D_A:D_A + D_B] = (zb * jax.nn.sigmoid(zb)).astype(BF16)

    _to_rowvreg(prv, POOL_HALO, pin_ref[...].astype(F32))
    grp_of_sublane = lax.broadcasted_iota(jnp.int32, (rv, LANES), 0) // (rv // N_POOL)
    window = jnp.left_shift(POOL_WINDOWS[0], grp_of_sublane)
    pos0 = start + i * tm

    def pool_body(it, carry):
        t0 = it * SEQ_TB
        base = pl.multiple_of((t0 + POOL_HALO - POOL_BUF) * rv, rv)
        x = [prv[pl.ds(base + j * rv, rv), :] for j in range(SEQ_TB + POOL_BUF)]
        for tt in range(SEQ_TB):
            cur = POOL_BUF + tt
            s = x[cur] + x[cur - 1]
            lo = 2
            for gi in range(1, N_POOL):
                hi = POOL_WINDOWS[gi]
                part = x[cur - lo]
                for j in range(lo + 1, hi):
                    part = part + x[cur - j]
                s = s + jnp.where(grp_of_sublane >= gi, part, 0.0)
                lo = hi
            cnt = jnp.minimum(pos0 + t0 + tt + 1, window).astype(F32)
            qrv[pl.ds(pl.multiple_of((t0 + tt) * rv, rv), rv), :] = s / cnt - x[cur]
        return carry

    lax.fori_loop(0, tm // SEQ_TB, pool_body, 0)
    acts_ref[:, D_A + D_B:D_A + D_B + D_C] = _from_rowvreg(qrv, tm).astype(BF16)


def _seq_p(grp, proj, p, layer, tm):
    n_tiles = grp.t_len // tm
    rv = ROW_VREGS

    def col(c):
        return pl.BlockSpec((tm, 1024), lambda n, i: (n * n_tiles + i, c))

    def lay(*shape):
        return pl.BlockSpec((None,) + shape, lambda n, i: (layer,) + (0,) * len(shape))

    return pl.pallas_call(
        functools.partial(_seq_p_kernel, tm=tm, n_tiles=n_tiles, start=0),
        grid=(grp.n_seq, n_tiles),
        in_specs=[col(COL_U), col(COL_V), col(COL_GA), col(COL_GB), col(COL_PIN),
                  lay(1, D_A), lay(1, D_A), lay(G_A, CHUNK, CHUNK), lay(CHUNK, D_A),
                  lay(K_CONV * rv, LANES), lay(rv, LANES), lay(1, D_B), lay(1, D_B)],
        out_specs=[pl.BlockSpec((tm, 3 * 1024), lambda n, i: (n * n_tiles + i, 0)),
                   pl.BlockSpec((None, K_CONV - 1, D_B), lambda n, i: (n, 0, 0))],
        out_shape=[jax.ShapeDtypeStruct((grp.rows, 3 * 1024), BF16),
                   jax.ShapeDtypeStruct((grp.n_seq, K_CONV - 1, D_B), F32)],
        scratch_shapes=[pltpu.VMEM(((CONV_HALO + tm) * rv, LANES), F32),
                        pltpu.VMEM((tm * rv, LANES), F32),
                        pltpu.VMEM(((POOL_HALO + tm) * rv, LANES), F32),
                        pltpu.VMEM((tm * rv, LANES), F32)],
        compiler_params=_params(("arbitrary", "arbitrary")),
        name="seq_prompt",
    )(proj, proj, proj, proj, proj, p["ln_v_g"], p["ln_v_b"], p["w_spatial"], p["b_mix"],
      p["w_dwconv_rv"], p["b_dwconv_rv"], p["ln_conv_g"], p["ln_conv_b"])


def _seq_s_kernel(u_ref, v_ref, ga_ref, gb_ref, pin_ref, sconv_ref, spool_ref,
                  lnvg_ref, lnvb_ref, wv_ref, bv_ref, wconv_ref, bconv_ref, lncg_ref, lncb_ref,
                  acts_ref, vn_ref, xb_ref, *, t_len, nb, start):
    vn = []
    for t in range(t_len):
        vn_t = _layernorm(_gelu(v_ref[t].astype(F32)), lnvg_ref[...], lnvb_ref[...])
        vn_ref[t] = vn_t
        vn.append(vn_t)
        mixed = jnp.broadcast_to(bv_ref[t:t + 1, :], (nb, D_A))
        for s in range(t + 1):
            mixed = mixed + wv_ref[t * t_len + s:t * t_len + s + 1, :] * vn[s]
        acts_ref[t, :, 0:D_A] = (_gelu(u_ref[t].astype(F32)) * mixed).astype(BF16)

    hist = K_CONV - 1
    acc = [jnp.broadcast_to(bconv_ref[...], (nb, D_B)) for _ in range(t_len)]
    for j in range(hist + t_len):
        if j < hist:
            xj = sconv_ref[j]
        else:
            xj = ga_ref[j - hist].astype(F32) * jax.nn.sigmoid(gb_ref[j - hist].astype(F32))
            xb_ref[j - hist] = xj
        for t in range(t_len):
            k = j - t
            if 0 <= k < K_CONV:
                acc[t] = acc[t] + wconv_ref[k:k + 1, :] * xj
    for t in range(t_len):
        zb = _layernorm(acc[t], lncg_ref[...], lncb_ref[...])
        acts_ref[t, :, D_A:D_A + D_B] = (zb * jax.nn.sigmoid(zb)).astype(BF16)

    rows = [spool_ref[j] for j in range(POOL_BUF)]
    rows += [pin_ref[t].astype(F32) for t in range(t_len)]
    for t in range(t_len):
        for gi, w in enumerate(POOL_WINDOWS):
            lanes = slice(gi * D_CG, (gi + 1) * D_CG)
            s = rows[POOL_BUF + t][:, lanes]
            for j in range(1, w):
                s = s + rows[POOL_BUF + t - j][:, lanes]
            cnt = float(min(start + t + 1, w))
            acts_ref[t, :, D_A + D_B + gi * D_CG:D_A + D_B + (gi + 1) * D_CG] = (
                s / cnt - rows[POOL_BUF + t][:, lanes]).astype(BF16)


def _seq_s(grp, proj, sconv_t, spool_t, p, layer, nb=32):
    t_len, n_seq = grp.t_len, grp.n_seq
    proj3 = proj.reshape(t_len, n_seq, N_IN)
    hist = K_CONV - 1

    def col(c):
        return pl.BlockSpec((t_len, nb, 1024), lambda b: (0, b, c))

    def lay(*shape):
        return pl.BlockSpec((None,) + shape, lambda b: (layer,) + (0,) * len(shape))

    return pl.pallas_call(
        functools.partial(_seq_s_kernel, t_len=t_len, nb=nb, start=PAST_LEN),
        grid=(n_seq // nb,),
        in_specs=[col(COL_U), col(COL_V), col(COL_GA), col(COL_GB), col(COL_PIN),
                  pl.BlockSpec((None, hist, nb, D_B), lambda b: (layer, 0, b, 0)),
                  pl.BlockSpec((None, POOL_BUF, nb, D_C), lambda b: (layer, 0, b, 0)),
                  lay(1, D_A), lay(1, D_A), lay(t_len * t_len, D_A), lay(t_len, D_A),
                  lay(K_CONV, D_B), lay(1, D_B), lay(1, D_B), lay(1, D_B)],
        out_specs=[pl.BlockSpec((t_len, nb, 3 * 1024), lambda b: (0, b, 0)),
                   pl.BlockSpec((t_len, nb, D_A), lambda b: (0, b, 0)),
                   pl.BlockSpec((t_len, nb, D_B), lambda b: (0, b, 0))],
        out_shape=[jax.ShapeDtypeStruct((t_len, n_seq, 3 * 1024), BF16),
                   jax.ShapeDtypeStruct((t_len, n_seq, D_A), F32),
                   jax.ShapeDtypeStruct((t_len, n_seq, D_B), F32)],
        compiler_params=_params(("arbitrary",)),
        name="seq_sample",
    )(proj3, proj3, proj3, proj3, proj3, sconv_t, spool_t,
      p["ln_v_g"], p["ln_v_b"], p["w_spatial_vec"], p["b_spatial_vec"],
      p["w_dwconv"], p["b_dwconv"], p["ln_conv_g"], p["ln_conv_b"])


def _mixmm_kernel(acts_ref, g0, g1, g2, g3, g4, g5, bgate_ref, wa_ref, wb_ref, wp_ref, ps_ref,
                  o_ref, wa_s, wb_s, wp_s):
    @pl.when(pl.program_id(0) == 0)
    def _():
        wa_s[...] = wa_ref[...].astype(BF16)
        wb_s[...] = wb_ref[...].astype(BF16)
        wp_s[...] = wp_ref[...].astype(BF16)

    gate_refs = (g0, g1, g2, g3, g4, g5)

    def gate(branch, half):
        idx = 2 * branch + half
        pre = gate_refs[idx][...].astype(F32) + bgate_ref[:, idx * 1024:(idx + 1) * 1024]
        return jax.nn.sigmoid(pre)

    ya = jnp.dot(acts_ref[:, 0:D_A], wa_s[...], preferred_element_type=F32)
    yb = jnp.dot(acts_ref[:, D_A:D_A + D_B], wb_s[...], preferred_element_type=F32)
    for half in range(2):
        cols = slice(half * 1024, (half + 1) * 1024)
        yc = jnp.concatenate(
            [jnp.dot(acts_ref[:, D_A + D_B + g * D_CG:D_A + D_B + (g + 1) * D_CG], wp_s[g],
                     preferred_element_type=F32) for g in (2 * half, 2 * half + 1)],
            axis=-1) * ps_ref[:, cols]
        mix = gate(0, half) * ya[:, cols] + gate(1, half) * yb[:, cols] + gate(2, half) * yc
        o_ref[:, cols] = mix.astype(BF16)


def _mixmm(acts, proj, p, layer, tm):
    m = acts.shape[0]

    def gcol(c):
        return pl.BlockSpec((tm, 1024), lambda i: (i, COL_GATES + c))

    def lay(*shape):
        return _resident((None,) + shape, lambda i: (layer,) + (0,) * len(shape))

    return pl.pallas_call(
        _mixmm_kernel,
        grid=(m // tm,),
        in_specs=[pl.BlockSpec((tm, 3 * 1024), lambda i: (i, 0))] + [gcol(c) for c in range(6)] + [
            lay(1, 3 * D_MODEL), lay(D_A, D_MODEL), lay(D_B, D_MODEL),
            lay(N_POOL, D_CG, D_CG_OUT), lay(1, D_MODEL)],
        out_specs=pl.BlockSpec((tm, D_MODEL), lambda i: (i, 0)),
        out_shape=jax.ShapeDtypeStruct((m, D_MODEL), BF16),
        scratch_shapes=[pltpu.VMEM((D_A, D_MODEL), BF16), pltpu.VMEM((D_B, D_MODEL), BF16),
                        pltpu.VMEM((N_POOL, D_CG, D_CG_OUT), BF16)],
        compiler_params=_params(("arbitrary",)),
        name="mixmm",
    )(acts, proj, proj, proj, proj, proj, proj, p["b_gate"], p["w_a_out"], p["w_b_out"],
      p["w_pool_grp"], p["pool_scale"])


def _g1_kernel(a_ref, w_ref, x_ref, gate_ref, gpost_ref, gnext_ref, sc_ref, sh_ref,
               xo_ref, ho_ref, *scratch):
    tm = x_ref.shape[0]
    if scratch:
        wb, = scratch

        @pl.when(pl.program_id(0) == 0)
        def _():
            wb[...] = w_ref[...].astype(BF16)

        w = wb[...]
    else:
        w = w_ref[...]
    y = jnp.dot(a_ref[...], w, preferred_element_type=F32)
    x = x_ref[...] + _rows(gate_ref[...], tm) * _rmsnorm(y, gpost_ref[...])
    xo_ref[...] = x
    h = _rmsnorm(x, gnext_ref[...]) * (1.0 + _rows(sc_ref[...], tm)) + _rows(sh_ref[...], tm)
    ho_ref[...] = h.astype(BF16)


def _g1(grp, a, w, x, layer, c_gate, g_post, layer_next, c_scale, c_shift, g_next, tm, name):
    k = a.shape[1]
    scratch = [pltpu.VMEM((k, D_MODEL), BF16)] if w.dtype == F32 else []
    return pl.pallas_call(
        _g1_kernel,
        grid=(grp.rows // tm,),
        in_specs=[
            pl.BlockSpec((tm, k), lambda i: (i, 0)),
            _resident((None, k, D_MODEL), lambda i: (layer, 0, 0)),
            pl.BlockSpec((tm, D_MODEL), lambda i: (i, 0)),
            grp.mod_spec(tm, layer, c_gate),
            _vec_spec(D_MODEL, layer),
            _vec_spec(D_MODEL, layer_next),
            grp.mod_spec(tm, layer_next, c_scale),
            grp.mod_spec(tm, layer_next, c_shift),
        ],
        out_specs=[pl.BlockSpec((tm, D_MODEL), lambda i: (i, 0)),
                   pl.BlockSpec((tm, D_MODEL), lambda i: (i, 0))],
        out_shape=[jax.ShapeDtypeStruct((grp.rows, D_MODEL), F32),
                   jax.ShapeDtypeStruct((grp.rows, D_MODEL), BF16)],
        scratch_shapes=scratch,
        compiler_params=_params(("arbitrary",)),
        name=name,
    )(a, w, x, grp.mods, _vec(g_post), _vec(g_next), grp.mods, grp.mods)


def _ffn_up_p_kernel(h_ref, wg_ref, wv0_ref, wv1_ref, cw_ref, cb_ref, act_ref, new_ref,
                     wb, carry, *, tm, tiles_per_seq):
    i = pl.program_id(1)
    seq_tile = i % tiles_per_seq

    @pl.when(i == 0)
    def _():
        wb[:, 0:FF_TILE] = wg_ref[...].astype(BF16)
        wb[:, FF_TILE:2 * FF_TILE - FF_SPLIT] = wv0_ref[:, FF_SPLIT:].astype(BF16)
        wb[:, 2 * FF_TILE - FF_SPLIT:] = wv1_ref[:, :FF_SPLIT].astype(BF16)

    @pl.when(seq_tile == 0)
    def _():
        carry[...] = jnp.zeros(carry.shape, F32)

    rc = ROW_CHUNK
    row = lax.broadcasted_iota(jnp.int32, (rc, 1), 0)
    w0, w1, w2, b = cw_ref[0:1, :], cw_ref[1:2, :], cw_ref[2:3, :], cb_ref[...]
    h1, h2 = carry[SUBLANES - 1:SUBLANES, :], carry[SUBLANES - 2:SUBLANES - 1, :]
    gp = None
    for q in range(tm // rc):
        rows = slice(q * rc, (q + 1) * rc)
        acc = jnp.dot(h_ref[rows, :], wb[...], preferred_element_type=F32)
        gp, val = acc[:, :FF_TILE], acc[:, FF_TILE:]
        prev1 = jnp.where(row == 0, h1, pltpu.roll(gp, 1, axis=0))
        prev2 = jnp.where(row == 0, h2, jnp.where(row == 1, h1, pltpu.roll(gp, 2, axis=0)))
        gc = w0 * prev2 + w1 * prev1 + w2 * gp + b
        act_ref[rows, :] = (_gelu(gc) * val).astype(BF16)
        h1, h2 = gp[rc - 1:rc, :], gp[rc - 2:rc - 1, :]
    carry[...] = gp[rc - SUBLANES:rc, :]

    @pl.when(seq_tile == tiles_per_seq - 1)
    def _():
        new_ref[...] = gp[rc - (K_FFN - 1):rc, :]


def _ffn_up_p(grp, h, w_up, p, layer, tm):
    tiles_per_seq = grp.t_len // tm
    n_j = pl.cdiv(D_FF, FF_TILE)
    val0 = D_FF // FF_TILE

    def wblk(off):
        return pl.BlockSpec((None, D_MODEL, FF_TILE), lambda j, i: (layer, 0, j + off))

    return pl.pallas_call(
        functools.partial(_ffn_up_p_kernel, tm=tm, tiles_per_seq=tiles_per_seq),
        grid=(n_j, grp.rows // tm),
        in_specs=[pl.BlockSpec((tm, D_MODEL), lambda j, i: (i, 0)),
                  wblk(0), wblk(val0), wblk(val0 + 1),
                  pl.BlockSpec((None, K_FFN, FF_TILE), lambda j, i: (layer, 0, j)),
                  pl.BlockSpec((None, 1, FF_TILE), lambda j, i: (layer, 0, j))],
        out_specs=[pl.BlockSpec((tm, FF_TILE), lambda j, i: (i, j)),
                   pl.BlockSpec((None, K_FFN - 1, FF_TILE),
                                lambda j, i: (i // tiles_per_seq, 0, j))],
        out_shape=[jax.ShapeDtypeStruct((grp.rows, D_FF), BF16),
                   jax.ShapeDtypeStruct((grp.n_seq, K_FFN - 1, D_FF), F32)],
        scratch_shapes=[pltpu.VMEM((D_MODEL, 2 * FF_TILE), BF16),
                        pltpu.VMEM((SUBLANES, FF_TILE), F32)],
        compiler_params=_params(("arbitrary", "arbitrary")),
        name="ffn_up_prompt",
    )(h, w_up, w_up, w_up, p["w_ffn_conv"], p["b_ffn_conv"])


def _ffn_act_s_kernel(gp_ref, val_ref, st_ref, w_ref, b_ref, o_ref, *, t_len):
    hist = K_FFN - 1
    rows = [st_ref[j] for j in range(hist)]
    rows += [gp_ref[t].astype(F32) for t in range(t_len)]
    for t in range(t_len):
        gc = b_ref[...] + w_ref[0:1, :] * rows[t]
        for k in range(1, K_FFN):
            gc = gc + w_ref[k:k + 1, :] * rows[t + k]
        o_ref[t] = (_gelu(gc) * val_ref[t].astype(F32)).astype(BF16)


def _ffn_act_s(grp, upv, sffn_t, p, layer, nb=32):
    t_len, n_seq = grp.t_len, grp.n_seq
    upv3 = upv.reshape(t_len, n_seq, 2 * D_FF)
    hist = K_FFN - 1

    def lay(*shape):
        return pl.BlockSpec((None,) + shape, lambda b: (layer,) + (0,) * len(shape))

    return pl.pallas_call(
        functools.partial(_ffn_act_s_kernel, t_len=t_len),
        grid=(n_seq // nb,),
        in_specs=[pl.BlockSpec((t_len, nb, D_FF), lambda b: (0, b, 0)),
                  pl.BlockSpec((t_len, nb, D_FF), lambda b: (0, b, 1)),
                  pl.BlockSpec((None, hist, nb, D_FF), lambda b: (layer, 0, b, 0)),
                  lay(K_FFN, D_FF), lay(1, D_FF)],
        out_specs=pl.BlockSpec((t_len, nb, D_FF), lambda b: (0, b, 0)),
        out_shape=jax.ShapeDtypeStruct((t_len, n_seq, D_FF), BF16),
        compiler_params=_params(("arbitrary",)),
        name="ffn_act_sample",
    )(upv3, upv3, sffn_t, p["w_ffn_conv"], p["b_ffn_conv"]).reshape(grp.rows, D_FF)


SHIFT_M, SCALE_M, GATE_M, SHIFT_F, SCALE_F, GATE_F = range(6)


def _small_params(p, t_s):
    ws, bs = p["w_spatial"], p["b_spatial"]
    b_mix = jnp.repeat(bs.transpose(0, 2, 1), CHUNK, axis=2)
    w_vec = jnp.repeat(ws[:, :, :t_s, :t_s].transpose(0, 2, 3, 1), CHUNK, axis=3)
    return dict(
        pool_scale=_vec(p["pool_scale"]), b_gate=_vec(p["b_gate"]),
        ln_v_g=_vec(p["ln_v_g"]), ln_v_b=_vec(p["ln_v_b"]),
        w_spatial=ws, b_mix=b_mix,
        w_spatial_vec=w_vec.reshape(DEPTH, t_s * t_s, D_A), b_spatial_vec=b_mix[:, :t_s],
        w_dwconv=p["w_dwconv"], b_dwconv=_vec(p["b_dwconv"]),
        w_dwconv_rv=p["w_dwconv"].reshape(DEPTH, K_CONV * ROW_VREGS, LANES),
        b_dwconv_rv=p["b_dwconv"].reshape(DEPTH, ROW_VREGS, LANES),
        ln_conv_g=_vec(p["ln_conv_g"]), ln_conv_b=_vec(p["ln_conv_b"]),
        w_ffn_conv=p["w_ffn_conv"], b_ffn_conv=_vec(p["b_ffn_conv"]),
        w_a_out=p["w_a_out"], w_b_out=p["w_b_out"], w_pool_grp=p["w_pool_grp"],
    )


def kernel(x_prompt, x_sample, c_prompt, c_sample, state_conv, state_pool, state_ffn_conv, ada_w, ada_b, g_pre_mix, g_post_mix, g_pre_ffn, g_post_ffn, w_in, b_gate, ln_v_g, ln_v_b, w_spatial, b_spatial, w_a_out, w_dwconv, b_dwconv, ln_conv_g, ln_conv_b, w_b_out, w_pool_grp, pool_scale, w_o, w_up, w_ffn_conv, b_ffn_conv, w_down):
    n_p, t_p, _ = x_prompt.shape
    n_s, t_s, _ = x_sample.shape
    sp = _small_params(dict(
        b_gate=b_gate, ln_v_g=ln_v_g, ln_v_b=ln_v_b, w_spatial=w_spatial, b_spatial=b_spatial,
        w_a_out=w_a_out, w_dwconv=w_dwconv, b_dwconv=b_dwconv, ln_conv_g=ln_conv_g,
        ln_conv_b=ln_conv_b, w_b_out=w_b_out, w_pool_grp=w_pool_grp, pool_scale=pool_scale,
        w_ffn_conv=w_ffn_conv, b_ffn_conv=b_ffn_conv), t_s)
    w_down_bf = w_down.astype(BF16)

    pad = (-(n_p + n_s)) % SUBLANES
    c_all = jnp.concatenate([c_sample, c_prompt, jnp.zeros((pad, D_MODEL), F32)], axis=0)
    mods = _mods(c_all, ada_w, ada_b)
    gs_ = _Group(n_s, t_s, True, mods)
    gp_ = _Group(n_p, t_p, False, mods[:, n_s:n_s + n_p].reshape(DEPTH, n_p, 1, 6 * D_MODEL))

    sconv_t = state_conv.transpose(0, 2, 1, 3)
    spool_t = state_pool.transpose(0, 2, 1, 3)
    sffn_t = state_ffn_conv.transpose(0, 2, 1, 3)

    x_p = x_prompt.reshape(n_p * t_p, D_MODEL)
    x_s = x_sample.transpose(1, 0, 2).reshape(t_s * n_s, D_MODEL)
    tm_mm, tm_s = t_p, t_s * n_s
    tm_seq = 256

    h_p = _prenorm(gp_, x_p, g_pre_mix, 0, SCALE_M, SHIFT_M, 512)
    h_s = _prenorm(gs_, x_s, g_pre_mix, 0, SCALE_M, SHIFT_M, tm_s)

    outs = {k: [] for k in ("conv_p", "xb_s", "pool_p", "pin_s", "ffn_p", "gp_s", "v_s")}
    pin_lo = COL_PIN * 1024
    for l in range(DEPTH):
        nxt = min(l + 1, DEPTH - 1)

        proj_p = _mm(h_p, w_in, l, tm_mm, 1024, "proj_prompt")
        proj_s = _mm(h_s, w_in, l, tm_s, 1024, "proj_sample")
        acts_p, nconv_p = _seq_p(gp_, proj_p, sp, l, tm_seq)
        acts_s, vn_s, xb_s = _seq_s(gs_, proj_s, sconv_t, spool_t, sp, l)
        mix_p = _mixmm(acts_p, proj_p, sp, l, tm_seq)
        mix_s = _mixmm(acts_s.reshape(gs_.rows, 3 * 1024), proj_s, sp, l, tm_seq)
        x_p, h_p = _g1(gp_, mix_p, w_o, x_p, l, GATE_M, g_post_mix, l, SCALE_F, SHIFT_F,
                       g_pre_ffn, tm_seq, "wo_prompt")
        x_s, h_s = _g1(gs_, mix_s, w_o, x_s, l, GATE_M, g_post_mix, l, SCALE_F, SHIFT_F,
                       g_pre_ffn, tm_s, "wo_sample")

        act_p, nffn_p = _ffn_up_p(gp_, h_p, w_up, sp, l, tm_mm)
        upv_s = _mm(h_s, w_up, l, tm_s, 1024, "up_sample")
        act_s = _ffn_act_s(gs_, upv_s, sffn_t, sp, l)
        x_p, h_p = _g1(gp_, act_p, w_down_bf, x_p, l, GATE_F, g_post_ffn, nxt, SCALE_M, SHIFT_M,
                       g_pre_mix, tm_seq, "down_prompt")
        x_s, h_s = _g1(gs_, act_s, w_down_bf, x_s, l, GATE_F, g_post_ffn, nxt, SCALE_M, SHIFT_M,
                       g_pre_mix, tm_s, "down_sample")

        outs["conv_p"].append(nconv_p)
        outs["xb_s"].append(xb_s)
        outs["pool_p"].append(
            proj_p.reshape(n_p, t_p, N_IN)[:, t_p - POOL_BUF:, pin_lo:pin_lo + D_C])
        outs["pin_s"].append(proj_s.reshape(t_s, n_s, N_IN)[:, :, pin_lo:pin_lo + D_C])
        outs["ffn_p"].append(nffn_p)
        outs["gp_s"].append(upv_s.reshape(t_s, n_s, 2 * D_FF)[t_s - (K_FFN - 1):, :, :D_FF])
        outs["v_s"].append(vn_s)

    def seq_major(parts):
        return jnp.stack(parts).astype(F32).transpose(0, 2, 1, 3)

    y_p = x_p.reshape(n_p, t_p, D_MODEL)
    y_s = x_s.reshape(t_s, n_s, D_MODEL).transpose(1, 0, 2)
    new_conv_s = jnp.concatenate([state_conv[:, :, t_s:], seq_major(outs["xb_s"])], axis=2)
    new_pool_s = jnp.concatenate([state_pool[:, :, t_s:], seq_major(outs["pin_s"])], axis=2)
    return (y_p, y_s, jnp.stack(outs["conv_p"]), new_conv_s,
            jnp.stack(outs["pool_p"]).astype(F32), new_pool_s,
            jnp.stack(outs["ffn_p"]).astype(F32), seq_major(outs["gp_s"]),
            seq_major(outs["v_s"]))
```

```python
import functools

import jax
import jax.numpy as jnp
from jax import lax
from jax.experimental import pallas as pl
from jax.experimental.pallas import tpu as pltpu

F32 = jnp.float32
BF16 = jnp.bfloat16

D_MODEL = 2048
DEPTH = 4
PAST_LEN = 16384
CHUNK = 128
D_A = D_MODEL // 2
D_B = D_MODEL // 2
D_C = D_MODEL // 2
G_A = D_A // CHUNK
K_CONV = 31
POOL_WINDOWS = (2, 4, 8, 16)
N_POOL = len(POOL_WINDOWS)
D_CG = D_C // N_POOL
D_CG_OUT = D_MODEL // N_POOL
POOL_BUF = max(POOL_WINDOWS) - 1
K_FFN = 3
D_FF = 5504
N_IN = 2 * D_A + 2 * D_B + D_C + 3 * D_MODEL
EPS = 1e-6

LANES = 128
SUBLANES = 8
ROW_VREGS = D_B // LANES

COL_U, COL_V, COL_GA, COL_GB, COL_PIN, COL_GATES = 0, 1, 2, 3, 4, 5

CONV_HALO = 32
POOL_HALO = 16
SEQ_TB = 16
CONV_PASS_TAPS = 16
ROW_CHUNK = 512
MM_CHUNK = 1024
FF_TILE = 512
FF_SPLIT = D_FF % FF_TILE

VMEM_LIMIT = 58 * 1024 * 1024


def _params(sem):
    return pltpu.CompilerParams(dimension_semantics=sem, vmem_limit_bytes=VMEM_LIMIT)


def _gelu(x):
    return jax.nn.gelu(x, approximate=True)


def _rmsnorm(x, g):
    return x * lax.rsqrt(jnp.mean(x * x, axis=-1, keepdims=True) + EPS) * g


def _layernorm(x, g, b):
    mu = jnp.mean(x, axis=-1, keepdims=True)
    xc = x - mu
    var = jnp.mean(xc * xc, axis=-1, keepdims=True)
    return xc * lax.rsqrt(var + EPS) * g + b


def _rows(m, tm):
    rm, c = m.shape
    if rm == 1 or rm == tm:
        return m
    return jnp.broadcast_to(m[None], (tm // rm, rm, c)).reshape(tm, c)


def _resident(shape, index_map):
    return pl.BlockSpec(shape, index_map, pipeline_mode=pl.Buffered(1))


def _mods_kernel(c_ref, w_ref, b_ref, o_ref):
    c = c_ref[...]
    a = (c * jax.nn.sigmoid(c)).astype(BF16)
    o_ref[...] = jnp.dot(a, w_ref[...].astype(BF16), preferred_element_type=F32) + b_ref[...]


def _mods(c_all, ada_w, ada_b, tn=1024):
    rows = c_all.shape[0]
    n_out = ada_w.shape[-1]
    return pl.pallas_call(
        _mods_kernel,
        grid=(DEPTH, n_out // tn),
        in_specs=[
            pl.BlockSpec((rows, D_MODEL), lambda l, j: (0, 0)),
            pl.BlockSpec((None, D_MODEL, tn), lambda l, j: (l, 0, j)),
            pl.BlockSpec((None, 1, tn), lambda l, j: (l, 0, j)),
        ],
        out_specs=pl.BlockSpec((None, rows, tn), lambda l, j: (l, 0, j)),
        out_shape=jax.ShapeDtypeStruct((DEPTH, rows, n_out), F32),
        compiler_params=_params(("arbitrary", "arbitrary")),
        name="mods",
    )(c_all, ada_w, ada_b.reshape(DEPTH, 1, n_out))


class _Group:
    def __init__(self, n_seq, t_len, time_major, mods):
        self.n_seq = n_seq
        self.t_len = t_len
        self.time_major = time_major
        self.rows = n_seq * t_len
        self.mods = mods

    def mod_spec(self, tm, layer, chunk):
        if self.time_major:
            return pl.BlockSpec((None, self.n_seq, D_MODEL), lambda i: (layer, 0, chunk))
        per_seq = self.t_len // tm
        return pl.BlockSpec((None, None, 1, D_MODEL), lambda i: (layer, i // per_seq, 0, chunk))


def _vec_spec(width, layer):
    return pl.BlockSpec((None, 1, width), lambda *_: (layer, 0, 0))


def _vec(p):
    return p.reshape(DEPTH, 1, p.shape[-1])


def _prenorm_kernel(x_ref, g_ref, sc_ref, sh_ref, h_ref):
    x = x_ref[...]
    tm = x.shape[0]
    y = _rmsnorm(x, g_ref[...])
    h_ref[...] = (y * (1.0 + _rows(sc_ref[...], tm)) + _rows(sh_ref[...], tm)).astype(BF16)


def _prenorm(grp, x, g, layer, c_scale, c_shift, tm):
    return pl.pallas_call(
        _prenorm_kernel,
        grid=(grp.rows // tm,),
        in_specs=[
            pl.BlockSpec((tm, D_MODEL), lambda i: (i, 0)),
            _vec_spec(D_MODEL, layer),
            grp.mod_spec(tm, layer, c_scale),
            grp.mod_spec(tm, layer, c_shift),
        ],
        out_specs=pl.BlockSpec((tm, D_MODEL), lambda i: (i, 0)),
        out_shape=jax.ShapeDtypeStruct((grp.rows, D_MODEL), BF16),
        compiler_params=_params(("arbitrary",)),
        name="prenorm",
    )(x, _vec(g), grp.mods, grp.mods)


def _mm_kernel(a_ref, as_ref, w_ref, o_ref, os_ref, wb, *, n_i):
    i = pl.program_id(1)

    @pl.when(i == 0)
    def _():
        wb[...] = w_ref[...].astype(BF16)

    @pl.when(i < n_i)
    def _():
        rc = min(MM_CHUNK, a_ref.shape[0])
        for q in range(a_ref.shape[0] // rc):
            rows = slice(q * rc, (q + 1) * rc)
            o_ref[rows, :] = jnp.dot(a_ref[rows, :], wb[...],
                                     preferred_element_type=F32).astype(BF16)

    @pl.when(i == n_i)
    def _():
        os_ref[...] = jnp.dot(as_ref[...], wb[...], preferred_element_type=F32).astype(BF16)


def _mm(a, a_s, w, layer, tm, tn, name):
    m, k = a.shape
    m_s = a_s.shape[0]
    n = w.shape[-1]
    n_i = m // tm
    return pl.pallas_call(
        functools.partial(_mm_kernel, n_i=n_i),
        grid=(pl.cdiv(n, tn), n_i + 1),
        in_specs=[
            pl.BlockSpec((tm, k), lambda j, i: (jnp.minimum(i, n_i - 1), 0)),
            _resident((m_s, k), lambda j, i: (0, 0)),
            pl.BlockSpec((None, k, tn), lambda j, i: (layer, 0, j)),
        ],
        out_specs=[pl.BlockSpec((tm, tn), lambda j, i: (jnp.minimum(i, n_i - 1), j)),
                   pl.BlockSpec((m_s, tn), lambda j, i: (0, j))],
        out_shape=[jax.ShapeDtypeStruct((m, n), BF16), jax.ShapeDtypeStruct((m_s, n), BF16)],
        scratch_shapes=[pltpu.VMEM((k, tn), BF16)],
        compiler_params=_params(("arbitrary", "arbitrary")),
        name=name,
    )(a, a_s, w)


def _to_rowvreg(ref, row0, x):
    tm = x.shape[0]
    for c in range(ROW_VREGS):
        ref[pl.ds(row0 * ROW_VREGS + c, tm, stride=ROW_VREGS), :] = x[:, c * LANES:(c + 1) * LANES]


def _from_rowvreg(ref, tm):
    return jnp.concatenate(
        [ref[pl.ds(c, tm, stride=ROW_VREGS), :] for c in range(ROW_VREGS)], axis=-1)


def _seq_p_kernel(u_ref, v_ref, ga_ref, gb_ref, pin_ref, lnvg_ref, lnvb_ref, ws_ref, bmix_ref,
                  wconv_ref, bconv_ref, lncg_ref, lncb_ref,
                  acts_ref, nconv_ref, xrv, yrv, prv, qrv, *, tm, n_tiles, start):
    i = pl.program_id(1)
    rv = ROW_VREGS

    vn = _layernorm(_gelu(v_ref[...].astype(F32)), lnvg_ref[...], lnvb_ref[...]).astype(BF16)
    gu = _gelu(u_ref[...].astype(F32))
    row = lax.broadcasted_iota(jnp.int32, (CHUNK, CHUNK), 0)
    col = lax.broadcasted_iota(jnp.int32, (CHUNK, CHUNK), 1)
    for g in range(G_A):
        lanes = slice(g * CHUNK, (g + 1) * CHUNK)
        wg = jnp.where(row >= col, ws_ref[g], 0.0).astype(BF16)
        for c in range(tm // CHUNK):
            rws = slice(c * CHUNK, (c + 1) * CHUNK)
            mixed = jnp.dot(wg, vn[rws, lanes], preferred_element_type=F32) + bmix_ref[:, lanes]
            acts_ref[rws, lanes] = (gu[rws, lanes] * mixed).astype(BF16)

    @pl.when(i == 0)
    def _():
        xrv[0:CONV_HALO * rv, :] = jnp.zeros((CONV_HALO * rv, LANES), F32)
        prv[0:POOL_HALO * rv, :] = jnp.zeros((POOL_HALO * rv, LANES), F32)

    @pl.when(i > 0)
    def _():
        xrv[0:CONV_HALO * rv, :] = xrv[tm * rv:(tm + CONV_HALO) * rv, :]
        prv[0:POOL_HALO * rv, :] = prv[tm * rv:(tm + POOL_HALO) * rv, :]

    xb = ga_ref[...].astype(F32) * jax.nn.sigmoid(gb_ref[...].astype(F32))
    _to_rowvreg(xrv, CONV_HALO, xb)

    @pl.when(i == n_tiles - 1)
    def _():
        nconv_ref[...] = xb[tm - (K_CONV - 1):tm, :]

    bias = bconv_ref[...]
    first = CONV_HALO - (K_CONV - 1)

    def conv_pass(k_lo, k_hi):
        taps = [wconv_ref[k * rv:(k + 1) * rv, :] for k in range(k_lo, k_hi)]

        def body(it, carry):
            t0 = it * SEQ_TB
            base = pl.multiple_of((t0 + first + k_lo) * rv, rv)
            outs = [pl.ds(pl.multiple_of((t0 + tt) * rv, rv), rv) for tt in range(SEQ_TB)]
            acc = [bias if k_lo == 0 else yrv[outs[tt], :] for tt in range(SEQ_TB)]
            for j in range(SEQ_TB + k_hi - k_lo - 1):
                xj = xrv[pl.ds(base + j * rv, rv), :]
                for tt in range(SEQ_TB):
                    k = j - tt
                    if 0 <= k < k_hi - k_lo:
                        acc[tt] = acc[tt] + taps[k] * xj
            for tt in range(SEQ_TB):
                yrv[outs[tt], :] = acc[tt]
            return carry

        lax.fori_loop(0, tm // SEQ_TB, body, 0)

    for k_lo in range(0, K_CONV, CONV_PASS_TAPS):
        conv_pass(k_lo, min(k_lo + CONV_PASS_TAPS, K_CONV))
    zb = _layernorm(_from_rowvreg(yrv, tm), lncg_ref[...], lncb_ref[...])
    acts_ref[:, D_A:D_A + D_B] = (zb * jax.nn.sigmoid(zb)).astype(BF16)

    _to_rowvreg(prv, POOL_HALO, pin_ref[...].astype(F32))
    grp_of_sublane = lax.broadcasted_iota(jnp.int32, (rv, LANES), 0) // (rv // N_POOL)
    window = jnp.left_shift(POOL_WINDOWS[0], grp_of_sublane)
    pos0 = start + i * tm

    def pool_body(it, carry):
        t0 = it * SEQ_TB
        base = pl.multiple_of((t0 + POOL_HALO - POOL_BUF) * rv, rv)
        x = [prv[pl.ds(base + j * rv, rv), :] for j in range(SEQ_TB + POOL_BUF)]
        for tt in range(SEQ_TB):
            cur = POOL_BUF + tt
            s = x[cur] + x[cur - 1]
            lo = 2
            for gi in range(1, N_POOL):
                hi = POOL_WINDOWS[gi]
                part = x[cur - lo]
                for j in range(lo + 1, hi):
                    part = part + x[cur - j]
                s = s + jnp.where(grp_of_sublane >= gi, part, 0.0)
                lo = hi
            cnt = jnp.minimum(pos0 + t0 + tt + 1, window).astype(F32)
            qrv[pl.ds(pl.multiple_of((t0 + tt) * rv, rv), rv), :] = s / cnt - x[cur]
        return carry

    lax.fori_loop(0, tm // SEQ_TB, pool_body, 0)
    acts_ref[:, D_A + D_B:D_A + D_B + D_C] = _from_rowvreg(qrv, tm).astype(BF16)


def _seq_p(grp, proj, p, layer, tm):
    n_tiles = grp.t_len // tm
    rv = ROW_VREGS

    def col(c):
        return pl.BlockSpec((tm, 1024), lambda n, i: (n * n_tiles + i, c))

    def lay(*shape):
        return pl.BlockSpec((None,) + shape, lambda n, i: (layer,) + (0,) * len(shape))

    return pl.pallas_call(
        functools.partial(_seq_p_kernel, tm=tm, n_tiles=n_tiles, start=0),
        grid=(grp.n_seq, n_tiles),
        in_specs=[col(COL_U), col(COL_V), col(COL_GA), col(COL_GB), col(COL_PIN),
                  lay(1, D_A), lay(1, D_A), lay(G_A, CHUNK, CHUNK), lay(CHUNK, D_A),
                  lay(K_CONV * rv, LANES), lay(rv, LANES), lay(1, D_B), lay(1, D_B)],
        out_specs=[pl.BlockSpec((tm, 3 * 1024), lambda n, i: (n * n_tiles + i, 0)),
                   pl.BlockSpec((None, K_CONV - 1, D_B), lambda n, i: (n, 0, 0))],
        out_shape=[jax.ShapeDtypeStruct((grp.rows, 3 * 1024), BF16),
                   jax.ShapeDtypeStruct((grp.n_seq, K_CONV - 1, D_B), F32)],
        scratch_shapes=[pltpu.VMEM(((CONV_HALO + tm) * rv, LANES), F32),
                        pltpu.VMEM((tm * rv, LANES), F32),
                        pltpu.VMEM(((POOL_HALO + tm) * rv, LANES), F32),
                        pltpu.VMEM((tm * rv, LANES), F32)],
        compiler_params=_params(("arbitrary", "arbitrary")),
        name="seq_prompt",
    )(proj, proj, proj, proj, proj, p["ln_v_g"], p["ln_v_b"], p["w_spatial"], p["b_mix"],
      p["w_dwconv_rv"], p["b_dwconv_rv"], p["ln_conv_g"], p["ln_conv_b"])


def _seq_s_kernel(u_ref, v_ref, ga_ref, gb_ref, pin_ref, sconv_ref, spool_ref,
                  lnvg_ref, lnvb_ref, wv_ref, bv_ref, wconv_ref, bconv_ref, lncg_ref, lncb_ref,
                  acts_ref, vn_ref, xb_ref, *, t_len, nb, start):
    vn = []
    for t in range(t_len):
        vn_t = _layernorm(_gelu(v_ref[t].astype(F32)), lnvg_ref[...], lnvb_ref[...])
        vn_ref[t] = vn_t
        vn.append(vn_t)
        mixed = jnp.broadcast_to(bv_ref[t:t + 1, :], (nb, D_A))
        for s in range(t + 1):
            mixed = mixed + wv_ref[t * t_len + s:t * t_len + s + 1, :] * vn[s]
        acts_ref[t, :, 0:D_A] = (_gelu(u_ref[t].astype(F32)) * mixed).astype(BF16)

    hist = K_CONV - 1
    acc = [jnp.broadcast_to(bconv_ref[...], (nb, D_B)) for _ in range(t_len)]
    for j in range(hist + t_len):
        if j < hist:
            xj = sconv_ref[j]
        else:
            xj = ga_ref[j - hist].astype(F32) * jax.nn.sigmoid(gb_ref[j - hist].astype(F32))
            xb_ref[j - hist] = xj
        for t in range(t_len):
            k = j - t
            if 0 <= k < K_CONV:
                acc[t] = acc[t] + wconv_ref[k:k + 1, :] * xj
    for t in range(t_len):
        zb = _layernorm(acc[t], lncg_ref[...], lncb_ref[...])
        acts_ref[t, :, D_A:D_A + D_B] = (zb * jax.nn.sigmoid(zb)).astype(BF16)

    rows = [spool_ref[j] for j in range(POOL_BUF)]
    rows += [pin_ref[t].astype(F32) for t in range(t_len)]
    for t in range(t_len):
        for gi, w in enumerate(POOL_WINDOWS):
            lanes = slice(gi * D_CG, (gi + 1) * D_CG)
            s = rows[POOL_BUF + t][:, lanes]
            for j in range(1, w):
                s = s + rows[POOL_BUF + t - j][:, lanes]
            cnt = float(min(start + t + 1, w))
            acts_ref[t, :, D_A + D_B + gi * D_CG:D_A + D_B + (gi + 1) * D_CG] = (
                s / cnt - rows[POOL_BUF + t][:, lanes]).astype(BF16)


def _seq_s(grp, proj, sconv_t, spool_t, p, layer, nb=32):
    t_len, n_seq = grp.t_len, grp.n_seq
    proj3 = proj.reshape(t_len, n_seq, N_IN)
    hist = K_CONV - 1

    def col(c):
        return pl.BlockSpec((t_len, nb, 1024), lambda b: (0, b, c))

    def lay(*shape):
        return pl.BlockSpec((None,) + shape, lambda b: (layer,) + (0,) * len(shape))

    return pl.pallas_call(
        functools.partial(_seq_s_kernel, t_len=t_len, nb=nb, start=PAST_LEN),
        grid=(n_seq // nb,),
        in_specs=[col(COL_U), col(COL_V), col(COL_GA), col(COL_GB), col(COL_PIN),
                  pl.BlockSpec((None, hist, nb, D_B), lambda b: (layer, 0, b, 0)),
                  pl.BlockSpec((None, POOL_BUF, nb, D_C), lambda b: (layer, 0, b, 0)),
                  lay(1, D_A), lay(1, D_A), lay(t_len * t_len, D_A), lay(t_len, D_A),
                  lay(K_CONV, D_B), lay(1, D_B), lay(1, D_B), lay(1, D_B)],
        out_specs=[pl.BlockSpec((t_len, nb, 3 * 1024), lambda b: (0, b, 0)),
                   pl.BlockSpec((t_len, nb, D_A), lambda b: (0, b, 0)),
                   pl.BlockSpec((t_len, nb, D_B), lambda b: (0, b, 0))],
        out_shape=[jax.ShapeDtypeStruct((t_len, n_seq, 3 * 1024), BF16),
                   jax.ShapeDtypeStruct((t_len, n_seq, D_A), F32),
                   jax.ShapeDtypeStruct((t_len, n_seq, D_B), F32)],
        compiler_params=_params(("arbitrary",)),
        name="seq_sample",
    )(proj3, proj3, proj3, proj3, proj3, sconv_t, spool_t,
      p["ln_v_g"], p["ln_v_b"], p["w_spatial_vec"], p["b_spatial_vec"],
      p["w_dwconv"], p["b_dwconv"], p["ln_conv_g"], p["ln_conv_b"])


def _mixmm_kernel(acts_ref, g0, g1, g2, g3, g4, g5, bgate_ref, wa_ref, wb_ref, wp_ref, ps_ref,
                  o_ref, wa_s, wb_s, wp_s):
    @pl.when(pl.program_id(0) == 0)
    def _():
        wa_s[...] = wa_ref[...].astype(BF16)
        wb_s[...] = wb_ref[...].astype(BF16)
        wp_s[...] = wp_ref[...].astype(BF16)

    gate_refs = (g0, g1, g2, g3, g4, g5)

    def gate(branch, half):
        idx = 2 * branch + half
        pre = gate_refs[idx][...].astype(F32) + bgate_ref[:, idx * 1024:(idx + 1) * 1024]
        return jax.nn.sigmoid(pre)

    ya = jnp.dot(acts_ref[:, 0:D_A], wa_s[...], preferred_element_type=F32)
    yb = jnp.dot(acts_ref[:, D_A:D_A + D_B], wb_s[...], preferred_element_type=F32)
    for half in range(2):
        cols = slice(half * 1024, (half + 1) * 1024)
        yc = jnp.concatenate(
            [jnp.dot(acts_ref[:, D_A + D_B + g * D_CG:D_A + D_B + (g + 1) * D_CG], wp_s[g],
                     preferred_element_type=F32) for g in (2 * half, 2 * half + 1)],
            axis=-1) * ps_ref[:, cols]
        mix = gate(0, half) * ya[:, cols] + gate(1, half) * yb[:, cols] + gate(2, half) * yc
        o_ref[:, cols] = mix.astype(BF16)


def _mixmm(acts, proj, p, layer, tm):
    m = acts.shape[0]

    def gcol(c):
        return pl.BlockSpec((tm, 1024), lambda i: (i, COL_GATES + c))

    def lay(*shape):
        return _resident((None,) + shape, lambda i: (layer,) + (0,) * len(shape))

    return pl.pallas_call(
        _mixmm_kernel,
        grid=(m // tm,),
        in_specs=[pl.BlockSpec((tm, 3 * 1024), lambda i: (i, 0))] + [gcol(c) for c in range(6)] + [
            lay(1, 3 * D_MODEL), lay(D_A, D_MODEL), lay(D_B, D_MODEL),
            lay(N_POOL, D_CG, D_CG_OUT), lay(1, D_MODEL)],
        out_specs=pl.BlockSpec((tm, D_MODEL), lambda i: (i, 0)),
        out_shape=jax.ShapeDtypeStruct((m, D_MODEL), BF16),
        scratch_shapes=[pltpu.VMEM((D_A, D_MODEL), BF16), pltpu.VMEM((D_B, D_MODEL), BF16),
                        pltpu.VMEM((N_POOL, D_CG, D_CG_OUT), BF16)],
        compiler_params=_params(("arbitrary",)),
        name="mixmm",
    )(acts, proj, proj, proj, proj, proj, proj, p["b_gate"], p["w_a_out"], p["w_b_out"],
      p["w_pool_grp"], p["pool_scale"])


def _g1_kernel(a_ref, w_ref, x_ref, gate_ref, gpost_ref, gnext_ref, sc_ref, sh_ref,
               xo_ref, ho_ref, *scratch):
    tm = x_ref.shape[0]
    if scratch:
        wb, = scratch

        @pl.when(pl.program_id(0) == 0)
        def _():
            wb[...] = w_ref[...].astype(BF16)

        w = wb[...]
    else:
        w = w_ref[...]
    y = jnp.dot(a_ref[...], w, preferred_element_type=F32)
    x = x_ref[...] + _rows(gate_ref[...], tm) * _rmsnorm(y, gpost_ref[...])
    xo_ref[...] = x
    h = _rmsnorm(x, gnext_ref[...]) * (1.0 + _rows(sc_ref[...], tm)) + _rows(sh_ref[...], tm)
    ho_ref[...] = h.astype(BF16)


def _g1(grp, a, w, x, layer, c_gate, g_post, layer_next, c_scale, c_shift, g_next, tm, name):
    k = a.shape[1]
    scratch = [pltpu.VMEM((k, D_MODEL), BF16)] if w.dtype == F32 else []
    return pl.pallas_call(
        _g1_kernel,
        grid=(grp.rows // tm,),
        in_specs=[
            pl.BlockSpec((tm, k), lambda i: (i, 0)),
            _resident((None, k, D_MODEL), lambda i: (layer, 0, 0)),
            pl.BlockSpec((tm, D_MODEL), lambda i: (i, 0)),
            grp.mod_spec(tm, layer, c_gate),
            _vec_spec(D_MODEL, layer),
            _vec_spec(D_MODEL, layer_next),
            grp.mod_spec(tm, layer_next, c_scale),
            grp.mod_spec(tm, layer_next, c_shift),
        ],
        out_specs=[pl.BlockSpec((tm, D_MODEL), lambda i: (i, 0)),
                   pl.BlockSpec((tm, D_MODEL), lambda i: (i, 0))],
        out_shape=[jax.ShapeDtypeStruct((grp.rows, D_MODEL), F32),
                   jax.ShapeDtypeStruct((grp.rows, D_MODEL), BF16)],
        scratch_shapes=scratch,
        compiler_params=_params(("arbitrary",)),
        name=name,
    )(a, w, x, grp.mods, _vec(g_post), _vec(g_next), grp.mods, grp.mods)


def _ffn_up_kernel(h_ref, hs_ref, st_ref, wg_ref, wv0_ref, wv1_ref, cw_ref, cb_ref,
                   act_ref, new_ref, acts_ref, news_ref, wb, carry,
                   *, tm, tiles_per_seq, n_i, t_s, n_s):
    i = pl.program_id(1)
    seq_tile = i % tiles_per_seq
    w0, w1, w2, b = cw_ref[0:1, :], cw_ref[1:2, :], cw_ref[2:3, :], cb_ref[...]

    @pl.when(i == 0)
    def _():
        wb[:, 0:FF_TILE] = wg_ref[...].astype(BF16)
        wb[:, FF_TILE:2 * FF_TILE - FF_SPLIT] = wv0_ref[:, FF_SPLIT:].astype(BF16)
        wb[:, 2 * FF_TILE - FF_SPLIT:] = wv1_ref[:, :FF_SPLIT].astype(BF16)

    @pl.when(seq_tile == 0)
    def _():
        carry[...] = jnp.zeros(carry.shape, F32)

    @pl.when(i < n_i)
    def _():
        rc = ROW_CHUNK
        row = lax.broadcasted_iota(jnp.int32, (rc, 1), 0)
        h1, h2 = carry[SUBLANES - 1:SUBLANES, :], carry[SUBLANES - 2:SUBLANES - 1, :]
        gp = None
        for q in range(tm // rc):
            rows = slice(q * rc, (q + 1) * rc)
            acc = jnp.dot(h_ref[rows, :], wb[...], preferred_element_type=F32)
            gp, val = acc[:, :FF_TILE], acc[:, FF_TILE:]
            prev1 = jnp.where(row == 0, h1, pltpu.roll(gp, 1, axis=0))
            prev2 = jnp.where(row == 0, h2, jnp.where(row == 1, h1, pltpu.roll(gp, 2, axis=0)))
            gc = w0 * prev2 + w1 * prev1 + w2 * gp + b
            act_ref[rows, :] = (_gelu(gc) * val).astype(BF16)
            h1, h2 = gp[rc - 1:rc, :], gp[rc - 2:rc - 1, :]
        carry[...] = gp[rc - SUBLANES:rc, :]

        @pl.when(seq_tile == tiles_per_seq - 1)
        def _():
            new_ref[...] = gp[rc - (K_FFN - 1):rc, :]

    @pl.when(i == n_i)
    def _():
        hist = K_FFN - 1
        acc = jnp.dot(hs_ref[...], wb[...], preferred_element_type=F32)
        gp, val = acc[:, :FF_TILE], acc[:, FF_TILE:]
        x = [st_ref[j] for j in range(hist)]
        x += [gp[t * n_s:(t + 1) * n_s, :] for t in range(t_s)]
        for t in range(t_s):
            gc = w0 * x[t] + w1 * x[t + 1] + w2 * x[t + 2] + b
            acts_ref[t * n_s:(t + 1) * n_s, :] = (
                _gelu(gc) * val[t * n_s:(t + 1) * n_s, :]).astype(BF16)
        for j in range(hist):
            news_ref[j] = x[t_s + j]


def _ffn_up(grp, grp_s, h, h_s, sffn_t, w_up, p, layer, tm):
    tiles_per_seq = grp.t_len // tm
    n_i = grp.rows // tm
    n_j = pl.cdiv(D_FF, FF_TILE)
    val0 = D_FF // FF_TILE
    hist = K_FFN - 1
    t_s, n_s = grp_s.t_len, grp_s.n_seq
    last = n_i - 1

    def wblk(off):
        return pl.BlockSpec((None, D_MODEL, FF_TILE), lambda j, i: (layer, 0, j + off))

    return pl.pallas_call(
        functools.partial(_ffn_up_kernel, tm=tm, tiles_per_seq=tiles_per_seq, n_i=n_i,
                          t_s=t_s, n_s=n_s),
        grid=(n_j, n_i + 1),
        in_specs=[pl.BlockSpec((tm, D_MODEL), lambda j, i: (jnp.minimum(i, last), 0)),
                  _resident((grp_s.rows, D_MODEL), lambda j, i: (0, 0)),
                  pl.BlockSpec((None, hist, n_s, FF_TILE), lambda j, i: (layer, 0, 0, j)),
                  wblk(0), wblk(val0), wblk(val0 + 1),
                  pl.BlockSpec((None, K_FFN, FF_TILE), lambda j, i: (layer, 0, j)),
                  pl.BlockSpec((None, 1, FF_TILE), lambda j, i: (layer, 0, j))],
        out_specs=[pl.BlockSpec((tm, FF_TILE), lambda j, i: (jnp.minimum(i, last), j)),
                   pl.BlockSpec((None, hist, FF_TILE),
                                lambda j, i: (jnp.minimum(i, last) // tiles_per_seq, 0, j)),
                   pl.BlockSpec((grp_s.rows, FF_TILE), lambda j, i: (0, j)),
                   pl.BlockSpec((hist, n_s, FF_TILE), lambda j, i: (0, 0, j))],
        out_shape=[jax.ShapeDtypeStruct((grp.rows, D_FF), BF16),
                   jax.ShapeDtypeStruct((grp.n_seq, hist, D_FF), F32),
                   jax.ShapeDtypeStruct((grp_s.rows, D_FF), BF16),
                   jax.ShapeDtypeStruct((hist, n_s, D_FF), F32)],
        scratch_shapes=[pltpu.VMEM((D_MODEL, 2 * FF_TILE), BF16),
                        pltpu.VMEM((SUBLANES, FF_TILE), F32)],
        compiler_params=_params(("arbitrary", "arbitrary")),
        name="ffn_up",
    )(h, h_s, sffn_t, w_up, w_up, w_up, p["w_ffn_conv"], p["b_ffn_conv"])


SHIFT_M, SCALE_M, GATE_M, SHIFT_F, SCALE_F, GATE_F = range(6)


def _small_params(p, t_s):
    ws, bs = p["w_spatial"], p["b_spatial"]
    b_mix = jnp.repeat(bs.transpose(0, 2, 1), CHUNK, axis=2)
    w_vec = jnp.repeat(ws[:, :, :t_s, :t_s].transpose(0, 2, 3, 1), CHUNK, axis=3)
    return dict(
        pool_scale=_vec(p["pool_scale"]), b_gate=_vec(p["b_gate"]),
        ln_v_g=_vec(p["ln_v_g"]), ln_v_b=_vec(p["ln_v_b"]),
        w_spatial=ws, b_mix=b_mix,
        w_spatial_vec=w_vec.reshape(DEPTH, t_s * t_s, D_A), b_spatial_vec=b_mix[:, :t_s],
        w_dwconv=p["w_dwconv"], b_dwconv=_vec(p["b_dwconv"]),
        w_dwconv_rv=p["w_dwconv"].reshape(DEPTH, K_CONV * ROW_VREGS, LANES),
        b_dwconv_rv=p["b_dwconv"].reshape(DEPTH, ROW_VREGS, LANES),
        ln_conv_g=_vec(p["ln_conv_g"]), ln_conv_b=_vec(p["ln_conv_b"]),
        w_ffn_conv=p["w_ffn_conv"], b_ffn_conv=_vec(p["b_ffn_conv"]),
        w_a_out=p["w_a_out"], w_b_out=p["w_b_out"], w_pool_grp=p["w_pool_grp"],
    )


def kernel(x_prompt, x_sample, c_prompt, c_sample, state_conv, state_pool, state_ffn_conv, ada_w, ada_b, g_pre_mix, g_post_mix, g_pre_ffn, g_post_ffn, w_in, b_gate, ln_v_g, ln_v_b, w_spatial, b_spatial, w_a_out, w_dwconv, b_dwconv, ln_conv_g, ln_conv_b, w_b_out, w_pool_grp, pool_scale, w_o, w_up, w_ffn_conv, b_ffn_conv, w_down):
    n_p, t_p, _ = x_prompt.shape
    n_s, t_s, _ = x_sample.shape
    sp = _small_params(dict(
        b_gate=b_gate, ln_v_g=ln_v_g, ln_v_b=ln_v_b, w_spatial=w_spatial, b_spatial=b_spatial,
        w_a_out=w_a_out, w_dwconv=w_dwconv, b_dwconv=b_dwconv, ln_conv_g=ln_conv_g,
        ln_conv_b=ln_conv_b, w_b_out=w_b_out, w_pool_grp=w_pool_grp, pool_scale=pool_scale,
        w_ffn_conv=w_ffn_conv, b_ffn_conv=b_ffn_conv), t_s)
    w_down_bf = w_down.astype(BF16)

    pad = (-(n_p + n_s)) % SUBLANES
    c_all = jnp.concatenate([c_sample, c_prompt, jnp.zeros((pad, D_MODEL), F32)], axis=0)
    mods = _mods(c_all, ada_w, ada_b)
    gs_ = _Group(n_s, t_s, True, mods)
    gp_ = _Group(n_p, t_p, False, mods[:, n_s:n_s + n_p].reshape(DEPTH, n_p, 1, 6 * D_MODEL))

    sconv_t = state_conv.transpose(0, 2, 1, 3)
    spool_t = state_pool.transpose(0, 2, 1, 3)
    sffn_t = state_ffn_conv.transpose(0, 2, 1, 3)

    x_p = x_prompt.reshape(n_p * t_p, D_MODEL)
    x_s = x_sample.transpose(1, 0, 2).reshape(t_s * n_s, D_MODEL)
    tm_mm, tm_s = t_p, t_s * n_s
    tm_seq = 256

    h_p = _prenorm(gp_, x_p, g_pre_mix, 0, SCALE_M, SHIFT_M, 512)
    h_s = _prenorm(gs_, x_s, g_pre_mix, 0, SCALE_M, SHIFT_M, tm_s)

    outs = {k: [] for k in ("conv_p", "xb_s", "pool_p", "pin_s", "ffn_p", "gp_s", "v_s")}
    pin_lo = COL_PIN * 1024
    for l in range(DEPTH):
        nxt = min(l + 1, DEPTH - 1)

        proj_p, proj_s = _mm(h_p, h_s, w_in, l, tm_mm, 1024, "proj")
        acts_p, nconv_p = _seq_p(gp_, proj_p, sp, l, tm_seq)
        acts_s, vn_s, xb_s = _seq_s(gs_, proj_s, sconv_t, spool_t, sp, l)
        mix_p = _mixmm(acts_p, proj_p, sp, l, tm_seq)
        mix_s = _mixmm(acts_s.reshape(gs_.rows, 3 * 1024), proj_s, sp, l, tm_seq)
        x_p, h_p = _g1(gp_, mix_p, w_o, x_p, l, GATE_M, g_post_mix, l, SCALE_F, SHIFT_F,
                       g_pre_ffn, tm_seq, "wo_prompt")
        x_s, h_s = _g1(gs_, mix_s, w_o, x_s, l, GATE_M, g_post_mix, l, SCALE_F, SHIFT_F,
                       g_pre_ffn, tm_s, "wo_sample")

        act_p, nffn_p, act_s, nffn_s = _ffn_up(gp_, gs_, h_p, h_s, sffn_t, w_up, sp, l, tm_mm)
        x_p, h_p = _g1(gp_, act_p, w_down_bf, x_p, l, GATE_F, g_post_ffn, nxt, SCALE_M, SHIFT_M,
                       g_pre_mix, tm_seq, "down_prompt")
        x_s, h_s = _g1(gs_, act_s, w_down_bf, x_s, l, GATE_F, g_post_ffn, nxt, SCALE_M, SHIFT_M,
                       g_pre_mix, tm_s, "down_sample")

        outs["conv_p"].append(nconv_p)
        outs["xb_s"].append(xb_s)
        outs["pool_p"].append(
            proj_p.reshape(n_p, t_p, N_IN)[:, t_p - POOL_BUF:, pin_lo:pin_lo + D_C])
        outs["pin_s"].append(proj_s.reshape(t_s, n_s, N_IN)[:, :, pin_lo:pin_lo + D_C])
        outs["ffn_p"].append(nffn_p)
        outs["gp_s"].append(nffn_s)
        outs["v_s"].append(vn_s)

    def seq_major(parts):
        return jnp.stack(parts).astype(F32).transpose(0, 2, 1, 3)

    y_p = x_p.reshape(n_p, t_p, D_MODEL)
    y_s = x_s.reshape(t_s, n_s, D_MODEL).transpose(1, 0, 2)
    new_conv_s = jnp.concatenate([state_conv[:, :, t_s:], seq_major(outs["xb_s"])], axis=2)
    new_pool_s = jnp.concatenate([state_pool[:, :, t_s:], seq_major(outs["pin_s"])], axis=2)
    return (y_p, y_s, jnp.stack(outs["conv_p"]), new_conv_s,
            jnp.stack(outs["pool_p"]).astype(F32), new_pool_s,
            jnp.stack(outs["ffn_p"]).astype(F32), seq_major(outs["gp_s"]),
            seq_major(outs["v_s"]))
```

```python
import functools

import jax
import jax.numpy as jnp
from jax import lax
from jax.experimental import pallas as pl
from jax.experimental.pallas import tpu as pltpu

F32 = jnp.float32
BF16 = jnp.bfloat16

D_MODEL = 2048
DEPTH = 4
PAST_LEN = 16384
CHUNK = 128
D_A = D_MODEL // 2
D_B = D_MODEL // 2
D_C = D_MODEL // 2
G_A = D_A // CHUNK
K_CONV = 31
POOL_WINDOWS = (2, 4, 8, 16)
N_POOL = len(POOL_WINDOWS)
D_CG = D_C // N_POOL
D_CG_OUT = D_MODEL // N_POOL
POOL_BUF = max(POOL_WINDOWS) - 1
K_FFN = 3
D_FF = 5504
N_IN = 2 * D_A + 2 * D_B + D_C + 3 * D_MODEL
EPS = 1e-6

LANES = 128
SUBLANES = 8
ROW_VREGS = D_B // LANES

COL_U, COL_V, COL_GA, COL_GB, COL_PIN, COL_GATES = 0, 1, 2, 3, 4, 5

CONV_HALO = 32
POOL_HALO = 16
SEQ_TB = 16
CONV_PASS_TAPS = 16
ROW_CHUNK = 512
MM_CHUNK = 1024
FF_TILE = 512
FF_SPLIT = D_FF % FF_TILE

VMEM_LIMIT = 58 * 1024 * 1024


def _params(sem):
    return pltpu.CompilerParams(dimension_semantics=sem, vmem_limit_bytes=VMEM_LIMIT)


def _gelu(x):
    return jax.nn.gelu(x, approximate=True)


def _rmsnorm(x, g):
    return x * lax.rsqrt(jnp.mean(x * x, axis=-1, keepdims=True) + EPS) * g


def _layernorm(x, g, b):
    mu = jnp.mean(x, axis=-1, keepdims=True)
    xc = x - mu
    var = jnp.mean(xc * xc, axis=-1, keepdims=True)
    return xc * lax.rsqrt(var + EPS) * g + b


def _rows(m, tm):
    rm, c = m.shape
    if rm == 1 or rm == tm:
        return m
    return jnp.broadcast_to(m[None], (tm // rm, rm, c)).reshape(tm, c)


def _resident(shape, index_map):
    return pl.BlockSpec(shape, index_map, pipeline_mode=pl.Buffered(1))


def _mods_kernel(c_ref, w_ref, b_ref, o_ref):
    c = c_ref[...]
    a = (c * jax.nn.sigmoid(c)).astype(BF16)
    o_ref[...] = jnp.dot(a, w_ref[...].astype(BF16), preferred_element_type=F32) + b_ref[...]


def _mods(c_all, ada_w, ada_b, tn=1024):
    rows = c_all.shape[0]
    n_out = ada_w.shape[-1]
    return pl.pallas_call(
        _mods_kernel,
        grid=(DEPTH, n_out // tn),
        in_specs=[
            pl.BlockSpec((rows, D_MODEL), lambda l, j: (0, 0)),
            pl.BlockSpec((None, D_MODEL, tn), lambda l, j: (l, 0, j)),
            pl.BlockSpec((None, 1, tn), lambda l, j: (l, 0, j)),
        ],
        out_specs=pl.BlockSpec((None, rows, tn), lambda l, j: (l, 0, j)),
        out_shape=jax.ShapeDtypeStruct((DEPTH, rows, n_out), F32),
        compiler_params=_params(("arbitrary", "arbitrary")),
        name="mods",
    )(c_all, ada_w, ada_b.reshape(DEPTH, 1, n_out))


class _Group:
    def __init__(self, n_seq, t_len, time_major, mods):
        self.n_seq = n_seq
        self.t_len = t_len
        self.time_major = time_major
        self.rows = n_seq * t_len
        self.mods = mods

    def mod_spec(self, tm, layer, chunk):
        if self.time_major:
            return pl.BlockSpec((None, self.n_seq, D_MODEL), lambda i: (layer, 0, chunk))
        per_seq = self.t_len // tm
        return pl.BlockSpec((None, None, 1, D_MODEL), lambda i: (layer, i // per_seq, 0, chunk))


def _vec_spec(width, layer):
    return pl.BlockSpec((None, 1, width), lambda *_: (layer, 0, 0))


def _vec(p):
    return p.reshape(DEPTH, 1, p.shape[-1])


def _prenorm_kernel(x_ref, g_ref, sc_ref, sh_ref, h_ref):
    x = x_ref[...]
    tm = x.shape[0]
    y = _rmsnorm(x, g_ref[...])
    h_ref[...] = (y * (1.0 + _rows(sc_ref[...], tm)) + _rows(sh_ref[...], tm)).astype(BF16)


def _prenorm(grp, x, g, layer, c_scale, c_shift, tm):
    return pl.pallas_call(
        _prenorm_kernel,
        grid=(grp.rows // tm,),
        in_specs=[
            pl.BlockSpec((tm, D_MODEL), lambda i: (i, 0)),
            _vec_spec(D_MODEL, layer),
            grp.mod_spec(tm, layer, c_scale),
            grp.mod_spec(tm, layer, c_shift),
        ],
        out_specs=pl.BlockSpec((tm, D_MODEL), lambda i: (i, 0)),
        out_shape=jax.ShapeDtypeStruct((grp.rows, D_MODEL), BF16),
        compiler_params=_params(("arbitrary",)),
        name="prenorm",
    )(x, _vec(g), grp.mods, grp.mods)


SAMPLE_STEP = 1


def _in_tile(i):
    return jnp.where(i <= SAMPLE_STEP, i, i - 1)


def _out_tile(i):
    return jnp.where(i < SAMPLE_STEP, i, i - 1)


def _mm_kernel(a_ref, as_ref, w_ref, o_ref, os_ref, wb):
    i = pl.program_id(1)

    @pl.when(i == 0)
    def _():
        wb[...] = w_ref[...].astype(BF16)

    @pl.when(i != SAMPLE_STEP)
    def _():
        rc = min(MM_CHUNK, a_ref.shape[0])
        for q in range(a_ref.shape[0] // rc):
            rows = slice(q * rc, (q + 1) * rc)
            o_ref[rows, :] = jnp.dot(a_ref[rows, :], wb[...],
                                     preferred_element_type=F32).astype(BF16)

    @pl.when(i == SAMPLE_STEP)
    def _():
        os_ref[...] = jnp.dot(as_ref[...], wb[...], preferred_element_type=F32).astype(BF16)


def _mm(a, a_s, w, layer, tm, tn, name):
    m, k = a.shape
    m_s = a_s.shape[0]
    n = w.shape[-1]
    n_i = m // tm
    assert n_i > SAMPLE_STEP
    return pl.pallas_call(
        _mm_kernel,
        grid=(pl.cdiv(n, tn), n_i + 1),
        in_specs=[
            pl.BlockSpec((tm, k), lambda j, i: (_in_tile(i), 0)),
            _resident((m_s, k), lambda j, i: (0, 0)),
            pl.BlockSpec((None, k, tn), lambda j, i: (layer, 0, j)),
        ],
        out_specs=[pl.BlockSpec((tm, tn), lambda j, i: (_out_tile(i), j)),
                   pl.BlockSpec((m_s, tn), lambda j, i: (0, j))],
        out_shape=[jax.ShapeDtypeStruct((m, n), BF16), jax.ShapeDtypeStruct((m_s, n), BF16)],
        scratch_shapes=[pltpu.VMEM((k, tn), BF16)],
        compiler_params=_params(("arbitrary", "arbitrary")),
        name=name,
    )(a, a_s, w)


def _to_rowvreg(ref, row0, x):
    tm = x.shape[0]
    for c in range(ROW_VREGS):
        ref[pl.ds(row0 * ROW_VREGS + c, tm, stride=ROW_VREGS), :] = x[:, c * LANES:(c + 1) * LANES]


def _from_rowvreg(ref, tm):
    return jnp.concatenate(
        [ref[pl.ds(c, tm, stride=ROW_VREGS), :] for c in range(ROW_VREGS)], axis=-1)


def _seq_p_kernel(u_ref, v_ref, ga_ref, gb_ref, pin_ref, lnvg_ref, lnvb_ref, ws_ref, bmix_ref,
                  wconv_ref, bconv_ref, lncg_ref, lncb_ref,
                  acts_ref, nconv_ref, xrv, yrv, prv, qrv, *, tm, n_tiles, start):
    i = pl.program_id(1)
    rv = ROW_VREGS

    vn = _layernorm(_gelu(v_ref[...].astype(F32)), lnvg_ref[...], lnvb_ref[...]).astype(BF16)
    gu = _gelu(u_ref[...].astype(F32))
    row = lax.broadcasted_iota(jnp.int32, (CHUNK, CHUNK), 0)
    col = lax.broadcasted_iota(jnp.int32, (CHUNK, CHUNK), 1)
    for g in range(G_A):
        lanes = slice(g * CHUNK, (g + 1) * CHUNK)
        wg = jnp.where(row >= col, ws_ref[g], 0.0).astype(BF16)
        for c in range(tm // CHUNK):
            rws = slice(c * CHUNK, (c + 1) * CHUNK)
            mixed = jnp.dot(wg, vn[rws, lanes], preferred_element_type=F32) + bmix_ref[:, lanes]
            acts_ref[rws, lanes] = (gu[rws, lanes] * mixed).astype(BF16)

    @pl.when(i == 0)
    def _():
        xrv[0:CONV_HALO * rv, :] = jnp.zeros((CONV_HALO * rv, LANES), F32)
        prv[0:POOL_HALO * rv, :] = jnp.zeros((POOL_HALO * rv, LANES), F32)

    @pl.when(i > 0)
    def _():
        xrv[0:CONV_HALO * rv, :] = xrv[tm * rv:(tm + CONV_HALO) * rv, :]
        prv[0:POOL_HALO * rv, :] = prv[tm * rv:(tm + POOL_HALO) * rv, :]

    xb = ga_ref[...].astype(F32) * jax.nn.sigmoid(gb_ref[...].astype(F32))
    _to_rowvreg(xrv, CONV_HALO, xb)

    @pl.when(i == n_tiles - 1)
    def _():
        nconv_ref[...] = xb[tm - (K_CONV - 1):tm, :]

    bias = bconv_ref[...]
    first = CONV_HALO - (K_CONV - 1)

    def conv_pass(k_lo, k_hi):
        taps = [wconv_ref[k * rv:(k + 1) * rv, :] for k in range(k_lo, k_hi)]

        def body(it, carry):
            t0 = it * SEQ_TB
            base = pl.multiple_of((t0 + first + k_lo) * rv, rv)
            outs = [pl.ds(pl.multiple_of((t0 + tt) * rv, rv), rv) for tt in range(SEQ_TB)]
            acc = [bias if k_lo == 0 else yrv[outs[tt], :] for tt in range(SEQ_TB)]
            for j in range(SEQ_TB + k_hi - k_lo - 1):
                xj = xrv[pl.ds(base + j * rv, rv), :]
                for tt in range(SEQ_TB):
                    k = j - tt
                    if 0 <= k < k_hi - k_lo:
                        acc[tt] = acc[tt] + taps[k] * xj
            for tt in range(SEQ_TB):
                yrv[outs[tt], :] = acc[tt]
            return carry

        lax.fori_loop(0, tm // SEQ_TB, body, 0)

    for k_lo in range(0, K_CONV, CONV_PASS_TAPS):
        conv_pass(k_lo, min(k_lo + CONV_PASS_TAPS, K_CONV))
    zb = _layernorm(_from_rowvreg(yrv, tm), lncg_ref[...], lncb_ref[...])
    acts_ref[:, D_A:D_A + D_B] = (zb * jax.nn.sigmoid(zb)).astype(BF16)

    _to_rowvreg(prv, POOL_HALO, pin_ref[...].astype(F32))
    grp_of_sublane = lax.broadcasted_iota(jnp.int32, (rv, LANES), 0) // (rv // N_POOL)
    window = jnp.left_shift(POOL_WINDOWS[0], grp_of_sublane)
    pos0 = start + i * tm

    def pool_body(it, carry):
        t0 = it * SEQ_TB
        base = pl.multiple_of((t0 + POOL_HALO - POOL_BUF) * rv, rv)
        x = [prv[pl.ds(base + j * rv, rv), :] for j in range(SEQ_TB + POOL_BUF)]
        for tt in range(SEQ_TB):
            cur = POOL_BUF + tt
            s = x[cur] + x[cur - 1]
            lo = 2
            for gi in range(1, N_POOL):
                hi = POOL_WINDOWS[gi]
                part = x[cur - lo]
                for j in range(lo + 1, hi):
                    part = part + x[cur - j]
                s = s + jnp.where(grp_of_sublane >= gi, part, 0.0)
                lo = hi
            cnt = jnp.minimum(pos0 + t0 + tt + 1, window).astype(F32)
            qrv[pl.ds(pl.multiple_of((t0 + tt) * rv, rv), rv), :] = s / cnt - x[cur]
        return carry

    lax.fori_loop(0, tm // SEQ_TB, pool_body, 0)
    acts_ref[:, D_A + D_B:D_A + D_B + D_C] = _from_rowvreg(qrv, tm).astype(BF16)


def _seq_p(grp, proj, p, layer, tm):
    n_tiles = grp.t_len // tm
    rv = ROW_VREGS

    def col(c):
        return pl.BlockSpec((tm, 1024), lambda n, i: (n * n_tiles + i, c))

    def lay(*shape):
        return pl.BlockSpec((None,) + shape, lambda n, i: (layer,) + (0,) * len(shape))

    return pl.pallas_call(
        functools.partial(_seq_p_kernel, tm=tm, n_tiles=n_tiles, start=0),
        grid=(grp.n_seq, n_tiles),
        in_specs=[col(COL_U), col(COL_V), col(COL_GA), col(COL_GB), col(COL_PIN),
                  lay(1, D_A), lay(1, D_A), lay(G_A, CHUNK, CHUNK), lay(CHUNK, D_A),
                  lay(K_CONV * rv, LANES), lay(rv, LANES), lay(1, D_B), lay(1, D_B)],
        out_specs=[pl.BlockSpec((tm, 3 * 1024), lambda n, i: (n * n_tiles + i, 0)),
                   pl.BlockSpec((None, K_CONV - 1, D_B), lambda n, i: (n, 0, 0))],
        out_shape=[jax.ShapeDtypeStruct((grp.rows, 3 * 1024), BF16),
                   jax.ShapeDtypeStruct((grp.n_seq, K_CONV - 1, D_B), F32)],
        scratch_shapes=[pltpu.VMEM(((CONV_HALO + tm) * rv, LANES), F32),
                        pltpu.VMEM((tm * rv, LANES), F32),
                        pltpu.VMEM(((POOL_HALO + tm) * rv, LANES), F32),
                        pltpu.VMEM((tm * rv, LANES), F32)],
        compiler_params=_params(("arbitrary", "arbitrary")),
        name="seq_prompt",
    )(proj, proj, proj, proj, proj, p["ln_v_g"], p["ln_v_b"], p["w_spatial"], p["b_mix"],
      p["w_dwconv_rv"], p["b_dwconv_rv"], p["ln_conv_g"], p["ln_conv_b"])


def _seq_s_kernel(u_ref, v_ref, ga_ref, gb_ref, pin_ref, sconv_ref, spool_ref,
                  lnvg_ref, lnvb_ref, wv_ref, bv_ref, wconv_ref, bconv_ref, lncg_ref, lncb_ref,
                  acts_ref, vn_ref, xb_ref, *, t_len, nb, start):
    vn = []
    for t in range(t_len):
        vn_t = _layernorm(_gelu(v_ref[t].astype(F32)), lnvg_ref[...], lnvb_ref[...])
        vn_ref[t] = vn_t
        vn.append(vn_t)
        mixed = jnp.broadcast_to(bv_ref[t:t + 1, :], (nb, D_A))
        for s in range(t + 1):
            mixed = mixed + wv_ref[t * t_len + s:t * t_len + s + 1, :] * vn[s]
        acts_ref[t, :, 0:D_A] = (_gelu(u_ref[t].astype(F32)) * mixed).astype(BF16)

    hist = K_CONV - 1
    acc = [jnp.broadcast_to(bconv_ref[...], (nb, D_B)) for _ in range(t_len)]
    for j in range(hist + t_len):
        if j < hist:
            xj = sconv_ref[j]
        else:
            xj = ga_ref[j - hist].astype(F32) * jax.nn.sigmoid(gb_ref[j - hist].astype(F32))
            xb_ref[j - hist] = xj
        for t in range(t_len):
            k = j - t
            if 0 <= k < K_CONV:
                acc[t] = acc[t] + wconv_ref[k:k + 1, :] * xj
    for t in range(t_len):
        zb = _layernorm(acc[t], lncg_ref[...], lncb_ref[...])
        acts_ref[t, :, D_A:D_A + D_B] = (zb * jax.nn.sigmoid(zb)).astype(BF16)

    rows = [spool_ref[j] for j in range(POOL_BUF)]
    rows += [pin_ref[t].astype(F32) for t in range(t_len)]
    for t in range(t_len):
        for gi, w in enumerate(POOL_WINDOWS):
            lanes = slice(gi * D_CG, (gi + 1) * D_CG)
            s = rows[POOL_BUF + t][:, lanes]
            for j in range(1, w):
                s = s + rows[POOL_BUF + t - j][:, lanes]
            cnt = float(min(start + t + 1, w))
            acts_ref[t, :, D_A + D_B + gi * D_CG:D_A + D_B + (gi + 1) * D_CG] = (
                s / cnt - rows[POOL_BUF + t][:, lanes]).astype(BF16)


def _seq_s(grp, proj, sconv_t, spool_t, p, layer, nb=32):
    t_len, n_seq = grp.t_len, grp.n_seq
    proj3 = proj.reshape(t_len, n_seq, N_IN)
    hist = K_CONV - 1

    def col(c):
        return pl.BlockSpec((t_len, nb, 1024), lambda b: (0, b, c))

    def lay(*shape):
        return pl.BlockSpec((None,) + shape, lambda b: (layer,) + (0,) * len(shape))

    return pl.pallas_call(
        functools.partial(_seq_s_kernel, t_len=t_len, nb=nb, start=PAST_LEN),
        grid=(n_seq // nb,),
        in_specs=[col(COL_U), col(COL_V), col(COL_GA), col(COL_GB), col(COL_PIN),
                  pl.BlockSpec((None, hist, nb, D_B), lambda b: (layer, 0, b, 0)),
                  pl.BlockSpec((None, POOL_BUF, nb, D_C), lambda b: (layer, 0, b, 0)),
                  lay(1, D_A), lay(1, D_A), lay(t_len * t_len, D_A), lay(t_len, D_A),
                  lay(K_CONV, D_B), lay(1, D_B), lay(1, D_B), lay(1, D_B)],
        out_specs=[pl.BlockSpec((t_len, nb, 3 * 1024), lambda b: (0, b, 0)),
                   pl.BlockSpec((t_len, nb, D_A), lambda b: (0, b, 0)),
                   pl.BlockSpec((t_len, nb, D_B), lambda b: (0, b, 0))],
        out_shape=[jax.ShapeDtypeStruct((t_len, n_seq, 3 * 1024), BF16),
                   jax.ShapeDtypeStruct((t_len, n_seq, D_A), F32),
                   jax.ShapeDtypeStruct((t_len, n_seq, D_B), F32)],
        compiler_params=_params(("arbitrary",)),
        name="seq_sample",
    )(proj3, proj3, proj3, proj3, proj3, sconv_t, spool_t,
      p["ln_v_g"], p["ln_v_b"], p["w_spatial_vec"], p["b_spatial_vec"],
      p["w_dwconv"], p["b_dwconv"], p["ln_conv_g"], p["ln_conv_b"])


def _mixmm_kernel(acts_ref, g0, g1, g2, g3, g4, g5, bgate_ref, wa_ref, wb_ref, wp_ref, ps_ref,
                  o_ref, wa_s, wb_s, wp_s):
    @pl.when(pl.program_id(0) == 0)
    def _():
        wa_s[...] = wa_ref[...].astype(BF16)
        wb_s[...] = wb_ref[...].astype(BF16)
        wp_s[...] = wp_ref[...].astype(BF16)

    gate_refs = (g0, g1, g2, g3, g4, g5)

    def gate(branch, half):
        idx = 2 * branch + half
        pre = gate_refs[idx][...].astype(F32) + bgate_ref[:, idx * 1024:(idx + 1) * 1024]
        return jax.nn.sigmoid(pre)

    ya = jnp.dot(acts_ref[:, 0:D_A], wa_s[...], preferred_element_type=F32)
    yb = jnp.dot(acts_ref[:, D_A:D_A + D_B], wb_s[...], preferred_element_type=F32)
    for half in range(2):
        cols = slice(half * 1024, (half + 1) * 1024)
        yc = jnp.concatenate(
            [jnp.dot(acts_ref[:, D_A + D_B + g * D_CG:D_A + D_B + (g + 1) * D_CG], wp_s[g],
                     preferred_element_type=F32) for g in (2 * half, 2 * half + 1)],
            axis=-1) * ps_ref[:, cols]
        mix = gate(0, half) * ya[:, cols] + gate(1, half) * yb[:, cols] + gate(2, half) * yc
        o_ref[:, cols] = mix.astype(BF16)


def _mixmm(acts, proj, p, layer, tm):
    m = acts.shape[0]

    def gcol(c):
        return pl.BlockSpec((tm, 1024), lambda i: (i, COL_GATES + c))

    def lay(*shape):
        return _resident((None,) + shape, lambda i: (layer,) + (0,) * len(shape))

    return pl.pallas_call(
        _mixmm_kernel,
        grid=(m // tm,),
        in_specs=[pl.BlockSpec((tm, 3 * 1024), lambda i: (i, 0))] + [gcol(c) for c in range(6)] + [
            lay(1, 3 * D_MODEL), lay(D_A, D_MODEL), lay(D_B, D_MODEL),
            lay(N_POOL, D_CG, D_CG_OUT), lay(1, D_MODEL)],
        out_specs=pl.BlockSpec((tm, D_MODEL), lambda i: (i, 0)),
        out_shape=jax.ShapeDtypeStruct((m, D_MODEL), BF16),
        scratch_shapes=[pltpu.VMEM((D_A, D_MODEL), BF16), pltpu.VMEM((D_B, D_MODEL), BF16),
                        pltpu.VMEM((N_POOL, D_CG, D_CG_OUT), BF16)],
        compiler_params=_params(("arbitrary",)),
        name="mixmm",
    )(acts, proj, proj, proj, proj, proj, proj, p["b_gate"], p["w_a_out"], p["w_b_out"],
      p["w_pool_grp"], p["pool_scale"])


def _g1_kernel(a_ref, w_ref, x_ref, gate_ref, gpost_ref, gnext_ref, sc_ref, sh_ref,
               xo_ref, ho_ref, *scratch):
    tm = x_ref.shape[0]
    if scratch:
        wb, = scratch

        @pl.when(pl.program_id(0) == 0)
        def _():
            wb[...] = w_ref[...].astype(BF16)

        w = wb[...]
    else:
        w = w_ref[...]
    y = jnp.dot(a_ref[...], w, preferred_element_type=F32)
    x = x_ref[...] + _rows(gate_ref[...], tm) * _rmsnorm(y, gpost_ref[...])
    xo_ref[...] = x
    h = _rmsnorm(x, gnext_ref[...]) * (1.0 + _rows(sc_ref[...], tm)) + _rows(sh_ref[...], tm)
    ho_ref[...] = h.astype(BF16)


def _g1(grp, a, w, x, layer, c_gate, g_post, layer_next, c_scale, c_shift, g_next, tm, name):
    k = a.shape[1]
    scratch = [pltpu.VMEM((k, D_MODEL), BF16)] if w.dtype == F32 else []
    return pl.pallas_call(
        _g1_kernel,
        grid=(grp.rows // tm,),
        in_specs=[
            pl.BlockSpec((tm, k), lambda i: (i, 0)),
            _resident((None, k, D_MODEL), lambda i: (layer, 0, 0)),
            pl.BlockSpec((tm, D_MODEL), lambda i: (i, 0)),
            grp.mod_spec(tm, layer, c_gate),
            _vec_spec(D_MODEL, layer),
            _vec_spec(D_MODEL, layer_next),
            grp.mod_spec(tm, layer_next, c_scale),
            grp.mod_spec(tm, layer_next, c_shift),
        ],
        out_specs=[pl.BlockSpec((tm, D_MODEL), lambda i: (i, 0)),
                   pl.BlockSpec((tm, D_MODEL), lambda i: (i, 0))],
        out_shape=[jax.ShapeDtypeStruct((grp.rows, D_MODEL), F32),
                   jax.ShapeDtypeStruct((grp.rows, D_MODEL), BF16)],
        scratch_shapes=scratch,
        compiler_params=_params(("arbitrary",)),
        name=name,
    )(a, w, x, grp.mods, _vec(g_post), _vec(g_next), grp.mods, grp.mods)


def _ffn_up_kernel(h_ref, hs_ref, st_ref, wg_ref, wv0_ref, wv1_ref, cw_ref, cb_ref,
                   act_ref, new_ref, acts_ref, news_ref, wb, carry,
                   *, tm, tiles_per_seq, t_s, n_s):
    i = pl.program_id(1)
    is_prompt = i != SAMPLE_STEP
    seq_tile = _out_tile(i) % tiles_per_seq
    w0, w1, w2, b = cw_ref[0:1, :], cw_ref[1:2, :], cw_ref[2:3, :], cb_ref[...]

    @pl.when(i == 0)
    def _():
        wb[:, 0:FF_TILE] = wg_ref[...].astype(BF16)
        wb[:, FF_TILE:2 * FF_TILE - FF_SPLIT] = wv0_ref[:, FF_SPLIT:].astype(BF16)
        wb[:, 2 * FF_TILE - FF_SPLIT:] = wv1_ref[:, :FF_SPLIT].astype(BF16)

    @pl.when(jnp.logical_and(is_prompt, seq_tile == 0))
    def _():
        carry[...] = jnp.zeros(carry.shape, F32)

    @pl.when(is_prompt)
    def _():
        rc = ROW_CHUNK
        row = lax.broadcasted_iota(jnp.int32, (rc, 1), 0)
        h1, h2 = carry[SUBLANES - 1:SUBLANES, :], carry[SUBLANES - 2:SUBLANES - 1, :]
        gp = None
        for q in range(tm // rc):
            rows = slice(q * rc, (q + 1) * rc)
            acc = jnp.dot(h_ref[rows, :], wb[...], preferred_element_type=F32)
            gp, val = acc[:, :FF_TILE], acc[:, FF_TILE:]
            prev1 = jnp.where(row == 0, h1, pltpu.roll(gp, 1, axis=0))
            prev2 = jnp.where(row == 0, h2, jnp.where(row == 1, h1, pltpu.roll(gp, 2, axis=0)))
            gc = w0 * prev2 + w1 * prev1 + w2 * gp + b
            act_ref[rows, :] = (_gelu(gc) * val).astype(BF16)
            h1, h2 = gp[rc - 1:rc, :], gp[rc - 2:rc - 1, :]
        carry[...] = gp[rc - SUBLANES:rc, :]

        @pl.when(seq_tile == tiles_per_seq - 1)
        def _():
            new_ref[...] = gp[rc - (K_FFN - 1):rc, :]

    @pl.when(i == SAMPLE_STEP)
    def _():
        hist = K_FFN - 1
        acc = jnp.dot(hs_ref[...], wb[...], preferred_element_type=F32)
        gp, val = acc[:, :FF_TILE], acc[:, FF_TILE:]
        x = [st_ref[j] for j in range(hist)]
        x += [gp[t * n_s:(t + 1) * n_s, :] for t in range(t_s)]
        for t in range(t_s):
            gc = w0 * x[t] + w1 * x[t + 1] + w2 * x[t + 2] + b
            acts_ref[t * n_s:(t + 1) * n_s, :] = (
                _gelu(gc) * val[t * n_s:(t + 1) * n_s, :]).astype(BF16)
        for j in range(hist):
            news_ref[j] = x[t_s + j]


def _ffn_up(grp, grp_s, h, h_s, sffn_t, w_up, p, layer, tm):
    tiles_per_seq = grp.t_len // tm
    n_i = grp.rows // tm
    n_j = pl.cdiv(D_FF, FF_TILE)
    val0 = D_FF // FF_TILE
    hist = K_FFN - 1
    t_s, n_s = grp_s.t_len, grp_s.n_seq
    assert n_i > SAMPLE_STEP

    def wblk(off):
        return pl.BlockSpec((None, D_MODEL, FF_TILE), lambda j, i: (layer, 0, j + off))

    return pl.pallas_call(
        functools.partial(_ffn_up_kernel, tm=tm, tiles_per_seq=tiles_per_seq, t_s=t_s, n_s=n_s),
        grid=(n_j, n_i + 1),
        in_specs=[pl.BlockSpec((tm, D_MODEL), lambda j, i: (_in_tile(i), 0)),
                  _resident((grp_s.rows, D_MODEL), lambda j, i: (0, 0)),
                  pl.BlockSpec((None, hist, n_s, FF_TILE), lambda j, i: (layer, 0, 0, j)),
                  wblk(0), wblk(val0), wblk(val0 + 1),
                  pl.BlockSpec((None, K_FFN, FF_TILE), lambda j, i: (layer, 0, j)),
                  pl.BlockSpec((None, 1, FF_TILE), lambda j, i: (layer, 0, j))],
        out_specs=[pl.BlockSpec((tm, FF_TILE), lambda j, i: (_out_tile(i), j)),
                   pl.BlockSpec((None, hist, FF_TILE),
                                lambda j, i: (_out_tile(i) // tiles_per_seq, 0, j)),
                   pl.BlockSpec((grp_s.rows, FF_TILE), lambda j, i: (0, j)),
                   pl.BlockSpec((hist, n_s, FF_TILE), lambda j, i: (0, 0, j))],
        out_shape=[jax.ShapeDtypeStruct((grp.rows, D_FF), BF16),
                   jax.ShapeDtypeStruct((grp.n_seq, hist, D_FF), F32),
                   jax.ShapeDtypeStruct((grp_s.rows, D_FF), BF16),
                   jax.ShapeDtypeStruct((hist, n_s, D_FF), F32)],
        scratch_shapes=[pltpu.VMEM((D_MODEL, 2 * FF_TILE), BF16),
                        pltpu.VMEM((SUBLANES, FF_TILE), F32)],
        compiler_params=_params(("arbitrary", "arbitrary")),
        name="ffn_up",
    )(h, h_s, sffn_t, w_up, w_up, w_up, p["w_ffn_conv"], p["b_ffn_conv"])


SHIFT_M, SCALE_M, GATE_M, SHIFT_F, SCALE_F, GATE_F = range(6)


def _small_params(p, t_s):
    ws, bs = p["w_spatial"], p["b_spatial"]
    b_mix = jnp.repeat(bs.transpose(0, 2, 1), CHUNK, axis=2)
    w_vec = jnp.repeat(ws[:, :, :t_s, :t_s].transpose(0, 2, 3, 1), CHUNK, axis=3)
    return dict(
        pool_scale=_vec(p["pool_scale"]), b_gate=_vec(p["b_gate"]),
        ln_v_g=_vec(p["ln_v_g"]), ln_v_b=_vec(p["ln_v_b"]),
        w_spatial=ws, b_mix=b_mix,
        w_spatial_vec=w_vec.reshape(DEPTH, t_s * t_s, D_A), b_spatial_vec=b_mix[:, :t_s],
        w_dwconv=p["w_dwconv"], b_dwconv=_vec(p["b_dwconv"]),
        w_dwconv_rv=p["w_dwconv"].reshape(DEPTH, K_CONV * ROW_VREGS, LANES),
        b_dwconv_rv=p["b_dwconv"].reshape(DEPTH, ROW_VREGS, LANES),
        ln_conv_g=_vec(p["ln_conv_g"]), ln_conv_b=_vec(p["ln_conv_b"]),
        w_ffn_conv=p["w_ffn_conv"], b_ffn_conv=_vec(p["b_ffn_conv"]),
        w_a_out=p["w_a_out"], w_b_out=p["w_b_out"], w_pool_grp=p["w_pool_grp"],
    )


def kernel(x_prompt, x_sample, c_prompt, c_sample, state_conv, state_pool, state_ffn_conv, ada_w, ada_b, g_pre_mix, g_post_mix, g_pre_ffn, g_post_ffn, w_in, b_gate, ln_v_g, ln_v_b, w_spatial, b_spatial, w_a_out, w_dwconv, b_dwconv, ln_conv_g, ln_conv_b, w_b_out, w_pool_grp, pool_scale, w_o, w_up, w_ffn_conv, b_ffn_conv, w_down):
    n_p, t_p, _ = x_prompt.shape
    n_s, t_s, _ = x_sample.shape
    sp = _small_params(dict(
        b_gate=b_gate, ln_v_g=ln_v_g, ln_v_b=ln_v_b, w_spatial=w_spatial, b_spatial=b_spatial,
        w_a_out=w_a_out, w_dwconv=w_dwconv, b_dwconv=b_dwconv, ln_conv_g=ln_conv_g,
        ln_conv_b=ln_conv_b, w_b_out=w_b_out, w_pool_grp=w_pool_grp, pool_scale=pool_scale,
        w_ffn_conv=w_ffn_conv, b_ffn_conv=b_ffn_conv), t_s)
    w_down_bf = w_down.astype(BF16)

    pad = (-(n_p + n_s)) % SUBLANES
    c_all = jnp.concatenate([c_sample, c_prompt, jnp.zeros((pad, D_MODEL), F32)], axis=0)
    mods = _mods(c_all, ada_w, ada_b)
    gs_ = _Group(n_s, t_s, True, mods)
    gp_ = _Group(n_p, t_p, False, mods[:, n_s:n_s + n_p].reshape(DEPTH, n_p, 1, 6 * D_MODEL))

    sconv_t = state_conv.transpose(0, 2, 1, 3)
    spool_t = state_pool.transpose(0, 2, 1, 3)
    sffn_t = state_ffn_conv.transpose(0, 2, 1, 3)

    x_p = x_prompt.reshape(n_p * t_p, D_MODEL)
    x_s = x_sample.transpose(1, 0, 2).reshape(t_s * n_s, D_MODEL)
    tm_mm, tm_s = t_p, t_s * n_s
    tm_seq = 256

    h_p = _prenorm(gp_, x_p, g_pre_mix, 0, SCALE_M, SHIFT_M, 512)
    h_s = _prenorm(gs_, x_s, g_pre_mix, 0, SCALE_M, SHIFT_M, tm_s)

    outs = {k: [] for k in ("conv_p", "xb_s", "pool_p", "pin_s", "ffn_p", "gp_s", "v_s")}
    pin_lo = COL_PIN * 1024
    for l in range(DEPTH):
        nxt = min(l + 1, DEPTH - 1)

        proj_p, proj_s = _mm(h_p, h_s, w_in, l, tm_mm, 1024, "proj")
        acts_p, nconv_p = _seq_p(gp_, proj_p, sp, l, tm_seq)
        acts_s, vn_s, xb_s = _seq_s(gs_, proj_s, sconv_t, spool_t, sp, l)
        mix_p = _mixmm(acts_p, proj_p, sp, l, tm_seq)
        mix_s = _mixmm(acts_s.reshape(gs_.rows, 3 * 1024), proj_s, sp, l, tm_seq)
        x_p, h_p = _g1(gp_, mix_p, w_o, x_p, l, GATE_M, g_post_mix, l, SCALE_F, SHIFT_F,
                       g_pre_ffn, tm_seq, "wo_prompt")
        x_s, h_s = _g1(gs_, mix_s, w_o, x_s, l, GATE_M, g_post_mix, l, SCALE_F, SHIFT_F,
                       g_pre_ffn, tm_s, "wo_sample")

        act_p, nffn_p, act_s, nffn_s = _ffn_up(gp_, gs_, h_p, h_s, sffn_t, w_up, sp, l, tm_mm)
        x_p, h_p = _g1(gp_, act_p, w_down_bf, x_p, l, GATE_F, g_post_ffn, nxt, SCALE_M, SHIFT_M,
                       g_pre_mix, tm_seq, "down_prompt")
        x_s, h_s = _g1(gs_, act_s, w_down_bf, x_s, l, GATE_F, g_post_ffn, nxt, SCALE_M, SHIFT_M,
                       g_pre_mix, tm_s, "down_sample")

        outs["conv_p"].append(nconv_p)
        outs["xb_s"].append(xb_s)
        outs["pool_p"].append(
            proj_p.reshape(n_p, t_p, N_IN)[:, t_p - POOL_BUF:, pin_lo:pin_lo + D_C])
        outs["pin_s"].append(proj_s.reshape(t_s, n_s, N_IN)[:, :, pin_lo:pin_lo + D_C])
        outs["ffn_p"].append(nffn_p)
        outs["gp_s"].append(nffn_s)
        outs["v_s"].append(vn_s)

    def seq_major(parts):
        return jnp.stack(parts).astype(F32).transpose(0, 2, 1, 3)

    y_p = x_p.reshape(n_p, t_p, D_MODEL)
    y_s = x_s.reshape(t_s, n_s, D_MODEL).transpose(1, 0, 2)
    new_conv_s = jnp.concatenate([state_conv[:, :, t_s:], seq_major(outs["xb_s"])], axis=2)
    new_pool_s = jnp.concatenate([state_pool[:, :, t_s:], seq_major(outs["pin_s"])], axis=2)
    return (y_p, y_s, jnp.stack(outs["conv_p"]), new_conv_s,
            jnp.stack(outs["pool_p"]).astype(F32), new_pool_s,
            jnp.stack(outs["ffn_p"]).astype(F32), seq_major(outs["gp_s"]),
            seq_major(outs["v_s"]))
```

```python
import functools

import jax
import jax.numpy as jnp
from jax import lax
from jax.experimental import pallas as pl
from jax.experimental.pallas import tpu as pltpu

F32 = jnp.float32
BF16 = jnp.bfloat16

D_MODEL = 2048
DEPTH = 4
PAST_LEN = 16384
CHUNK = 128
D_A = D_MODEL // 2
D_B = D_MODEL // 2
D_C = D_MODEL // 2
G_A = D_A // CHUNK
K_CONV = 31
POOL_WINDOWS = (2, 4, 8, 16)
N_POOL = len(POOL_WINDOWS)
D_CG = D_C // N_POOL
D_CG_OUT = D_MODEL // N_POOL
POOL_BUF = max(POOL_WINDOWS) - 1
K_FFN = 3
D_FF = 5504
N_IN = 2 * D_A + 2 * D_B + D_C + 3 * D_MODEL
EPS = 1e-6

LANES = 128
SUBLANES = 8
ROW_VREGS = D_B // LANES

COL_U, COL_V, COL_GA, COL_GB, COL_PIN, COL_GATES = 0, 1, 2, 3, 4, 5

CONV_HALO = 32
POOL_HALO = 16
SEQ_TB = 16
CONV_PASS_TAPS = 16
ROW_CHUNK = 512
MM_CHUNK = 1024
FF_TILE = 512
FF_SPLIT = D_FF % FF_TILE

VMEM_LIMIT = 58 * 1024 * 1024


def _params(sem):
    return pltpu.CompilerParams(dimension_semantics=sem, vmem_limit_bytes=VMEM_LIMIT)


def _gelu(x):
    return jax.nn.gelu(x, approximate=True)


def _rmsnorm(x, g):
    return x * lax.rsqrt(jnp.mean(x * x, axis=-1, keepdims=True) + EPS) * g


def _layernorm(x, g, b):
    mu = jnp.mean(x, axis=-1, keepdims=True)
    xc = x - mu
    var = jnp.mean(xc * xc, axis=-1, keepdims=True)
    return xc * lax.rsqrt(var + EPS) * g + b


def _rows(m, tm):
    rm, c = m.shape
    if rm == 1 or rm == tm:
        return m
    return jnp.broadcast_to(m[None], (tm // rm, rm, c)).reshape(tm, c)


def _resident(shape, index_map):
    return pl.BlockSpec(shape, index_map, pipeline_mode=pl.Buffered(1))


def _mods_kernel(c_ref, w_ref, b_ref, o_ref):
    c = c_ref[...]
    a = (c * jax.nn.sigmoid(c)).astype(BF16)
    o_ref[...] = jnp.dot(a, w_ref[...].astype(BF16), preferred_element_type=F32) + b_ref[...]


def _mods(c_all, ada_w, ada_b, tn=1024):
    rows = c_all.shape[0]
    n_out = ada_w.shape[-1]
    return pl.pallas_call(
        _mods_kernel,
        grid=(DEPTH, n_out // tn),
        in_specs=[
            pl.BlockSpec((rows, D_MODEL), lambda l, j: (0, 0)),
            pl.BlockSpec((None, D_MODEL, tn), lambda l, j: (l, 0, j)),
            pl.BlockSpec((None, 1, tn), lambda l, j: (l, 0, j)),
        ],
        out_specs=pl.BlockSpec((None, rows, tn), lambda l, j: (l, 0, j)),
        out_shape=jax.ShapeDtypeStruct((DEPTH, rows, n_out), F32),
        compiler_params=_params(("arbitrary", "arbitrary")),
        name="mods",
    )(c_all, ada_w, ada_b.reshape(DEPTH, 1, n_out))


class _Group:
    def __init__(self, n_seq, t_len, time_major, mods):
        self.n_seq = n_seq
        self.t_len = t_len
        self.time_major = time_major
        self.rows = n_seq * t_len
        self.mods = mods

    def mod_spec(self, tm, layer, chunk):
        if self.time_major:
            return pl.BlockSpec((None, self.n_seq, D_MODEL), lambda i: (layer, 0, chunk))
        per_seq = self.t_len // tm
        return pl.BlockSpec((None, None, 1, D_MODEL), lambda i: (layer, i // per_seq, 0, chunk))


def _vec_spec(width, layer):
    return pl.BlockSpec((None, 1, width), lambda *_: (layer, 0, 0))


def _vec(p):
    return p.reshape(DEPTH, 1, p.shape[-1])


def _prenorm_kernel(x_ref, g_ref, sc_ref, sh_ref, h_ref):
    x = x_ref[...]
    tm = x.shape[0]
    y = _rmsnorm(x, g_ref[...])
    h_ref[...] = (y * (1.0 + _rows(sc_ref[...], tm)) + _rows(sh_ref[...], tm)).astype(BF16)


def _prenorm(grp, x, g, layer, c_scale, c_shift, tm):
    return pl.pallas_call(
        _prenorm_kernel,
        grid=(grp.rows // tm,),
        in_specs=[
            pl.BlockSpec((tm, D_MODEL), lambda i: (i, 0)),
            _vec_spec(D_MODEL, layer),
            grp.mod_spec(tm, layer, c_scale),
            grp.mod_spec(tm, layer, c_shift),
        ],
        out_specs=pl.BlockSpec((tm, D_MODEL), lambda i: (i, 0)),
        out_shape=jax.ShapeDtypeStruct((grp.rows, D_MODEL), BF16),
        compiler_params=_params(("arbitrary",)),
        name="prenorm",
    )(x, _vec(g), grp.mods, grp.mods)


SAMPLE_STEP = 1


def _in_tile(i):
    return jnp.where(i <= SAMPLE_STEP, i, i - 1)


def _out_tile(i):
    return jnp.where(i < SAMPLE_STEP, i, i - 1)


def _mm_kernel(a_ref, as_ref, w_ref, o_ref, os_ref, wb):
    i = pl.program_id(1)

    @pl.when(i == 0)
    def _():
        wb[...] = w_ref[...].astype(BF16)

    @pl.when(i != SAMPLE_STEP)
    def _():
        rc = min(MM_CHUNK, a_ref.shape[0])
        for q in range(a_ref.shape[0] // rc):
            rows = slice(q * rc, (q + 1) * rc)
            o_ref[rows, :] = jnp.dot(a_ref[rows, :], wb[...],
                                     preferred_element_type=F32).astype(BF16)

    @pl.when(i == SAMPLE_STEP)
    def _():
        os_ref[...] = jnp.dot(as_ref[...], wb[...], preferred_element_type=F32).astype(BF16)


def _mm(a, a_s, w, layer, tm, tn, name):
    m, k = a.shape
    m_s = a_s.shape[0]
    n = w.shape[-1]
    n_i = m // tm
    assert n_i > SAMPLE_STEP
    return pl.pallas_call(
        _mm_kernel,
        grid=(pl.cdiv(n, tn), n_i + 1),
        in_specs=[
            pl.BlockSpec((tm, k), lambda j, i: (_in_tile(i), 0)),
            _resident((m_s, k), lambda j, i: (0, 0)),
            pl.BlockSpec((None, k, tn), lambda j, i: (layer, 0, j)),
        ],
        out_specs=[pl.BlockSpec((tm, tn), lambda j, i: (_out_tile(i), j)),
                   pl.BlockSpec((m_s, tn), lambda j, i: (0, j))],
        out_shape=[jax.ShapeDtypeStruct((m, n), BF16), jax.ShapeDtypeStruct((m_s, n), BF16)],
        scratch_shapes=[pltpu.VMEM((k, tn), BF16)],
        compiler_params=_params(("arbitrary", "arbitrary")),
        name=name,
    )(a, a_s, w)


def _to_rowvreg(ref, row0, x):
    tm = x.shape[0]
    for c in range(ROW_VREGS):
        ref[pl.ds(row0 * ROW_VREGS + c, tm, stride=ROW_VREGS), :] = x[:, c * LANES:(c + 1) * LANES]


def _from_rowvreg(ref, tm):
    return jnp.concatenate(
        [ref[pl.ds(c, tm, stride=ROW_VREGS), :] for c in range(ROW_VREGS)], axis=-1)


def _seq_p_kernel(u_ref, v_ref, ga_ref, gb_ref, pin_ref, lnvg_ref, lnvb_ref, ws_ref, bmix_ref,
                  wconv_ref, bconv_ref, lncg_ref, lncb_ref,
                  acts_ref, nconv_ref, xrv, yrv, prv, qrv, *, tm, n_tiles, start):
    i = pl.program_id(1)
    rv = ROW_VREGS

    vn = _layernorm(_gelu(v_ref[...].astype(F32)), lnvg_ref[...], lnvb_ref[...]).astype(BF16)
    gu = _gelu(u_ref[...].astype(F32))
    row = lax.broadcasted_iota(jnp.int32, (CHUNK, CHUNK), 0)
    col = lax.broadcasted_iota(jnp.int32, (CHUNK, CHUNK), 1)
    for g in range(G_A):
        lanes = slice(g * CHUNK, (g + 1) * CHUNK)
        wg = jnp.where(row >= col, ws_ref[g], 0.0).astype(BF16)
        for c in range(tm // CHUNK):
            rws = slice(c * CHUNK, (c + 1) * CHUNK)
            mixed = jnp.dot(wg, vn[rws, lanes], preferred_element_type=F32) + bmix_ref[:, lanes]
            acts_ref[rws, lanes] = (gu[rws, lanes] * mixed).astype(BF16)

    @pl.when(i == 0)
    def _():
        xrv[0:CONV_HALO * rv, :] = jnp.zeros((CONV_HALO * rv, LANES), F32)
        prv[0:POOL_HALO * rv, :] = jnp.zeros((POOL_HALO * rv, LANES), F32)

    @pl.when(i > 0)
    def _():
        xrv[0:CONV_HALO * rv, :] = xrv[tm * rv:(tm + CONV_HALO) * rv, :]
        prv[0:POOL_HALO * rv, :] = prv[tm * rv:(tm + POOL_HALO) * rv, :]

    xb = ga_ref[...].astype(F32) * jax.nn.sigmoid(gb_ref[...].astype(F32))
    _to_rowvreg(xrv, CONV_HALO, xb)

    @pl.when(i == n_tiles - 1)
    def _():
        nconv_ref[...] = xb[tm - (K_CONV - 1):tm, :]

    bias = bconv_ref[...]
    first = CONV_HALO - (K_CONV - 1)

    def conv_pass(k_lo, k_hi):
        taps = [wconv_ref[k * rv:(k + 1) * rv, :] for k in range(k_lo, k_hi)]

        def body(it, carry):
            t0 = it * SEQ_TB
            base = pl.multiple_of((t0 + first + k_lo) * rv, rv)
            outs = [pl.ds(pl.multiple_of((t0 + tt) * rv, rv), rv) for tt in range(SEQ_TB)]
            acc = [bias if k_lo == 0 else yrv[outs[tt], :] for tt in range(SEQ_TB)]
            for j in range(SEQ_TB + k_hi - k_lo - 1):
                xj = xrv[pl.ds(base + j * rv, rv), :]
                for tt in range(SEQ_TB):
                    k = j - tt
                    if 0 <= k < k_hi - k_lo:
                        acc[tt] = acc[tt] + taps[k] * xj
            for tt in range(SEQ_TB):
                yrv[outs[tt], :] = acc[tt]
            return carry

        lax.fori_loop(0, tm // SEQ_TB, body, 0, unroll=True)

    for k_lo in range(0, K_CONV, CONV_PASS_TAPS):
        conv_pass(k_lo, min(k_lo + CONV_PASS_TAPS, K_CONV))
    zb = _layernorm(_from_rowvreg(yrv, tm), lncg_ref[...], lncb_ref[...])
    acts_ref[:, D_A:D_A + D_B] = (zb * jax.nn.sigmoid(zb)).astype(BF16)

    _to_rowvreg(prv, POOL_HALO, pin_ref[...].astype(F32))
    grp_of_sublane = lax.broadcasted_iota(jnp.int32, (rv, LANES), 0) // (rv // N_POOL)
    window = jnp.left_shift(POOL_WINDOWS[0], grp_of_sublane)
    pos0 = start + i * tm

    def pool_body(it, carry):
        t0 = it * SEQ_TB
        base = pl.multiple_of((t0 + POOL_HALO - POOL_BUF) * rv, rv)
        x = [prv[pl.ds(base + j * rv, rv), :] for j in range(SEQ_TB + POOL_BUF)]
        for tt in range(SEQ_TB):
            cur = POOL_BUF + tt
            s = x[cur] + x[cur - 1]
            lo = 2
            for gi in range(1, N_POOL):
                hi = POOL_WINDOWS[gi]
                part = x[cur - lo]
                for j in range(lo + 1, hi):
                    part = part + x[cur - j]
                s = s + jnp.where(grp_of_sublane >= gi, part, 0.0)
                lo = hi
            cnt = jnp.minimum(pos0 + t0 + tt + 1, window).astype(F32)
            qrv[pl.ds(pl.multiple_of((t0 + tt) * rv, rv), rv), :] = s / cnt - x[cur]
        return carry

    lax.fori_loop(0, tm // SEQ_TB, pool_body, 0, unroll=True)
    acts_ref[:, D_A + D_B:D_A + D_B + D_C] = _from_rowvreg(qrv, tm).astype(BF16)


def _seq_p(grp, proj, p, layer, tm):
    n_tiles = grp.t_len // tm
    rv = ROW_VREGS

    def col(c):
        return pl.BlockSpec((tm, 1024), lambda n, i: (n * n_tiles + i, c))

    def lay(*shape):
        return pl.BlockSpec((None,) + shape, lambda n, i: (layer,) + (0,) * len(shape))

    return pl.pallas_call(
        functools.partial(_seq_p_kernel, tm=tm, n_tiles=n_tiles, start=0),
        grid=(grp.n_seq, n_tiles),
        in_specs=[col(COL_U), col(COL_V), col(COL_GA), col(COL_GB), col(COL_PIN),
                  lay(1, D_A), lay(1, D_A), lay(G_A, CHUNK, CHUNK), lay(CHUNK, D_A),
                  lay(K_CONV * rv, LANES), lay(rv, LANES), lay(1, D_B), lay(1, D_B)],
        out_specs=[pl.BlockSpec((tm, 3 * 1024), lambda n, i: (n * n_tiles + i, 0)),
                   pl.BlockSpec((None, K_CONV - 1, D_B), lambda n, i: (n, 0, 0))],
        out_shape=[jax.ShapeDtypeStruct((grp.rows, 3 * 1024), BF16),
                   jax.ShapeDtypeStruct((grp.n_seq, K_CONV - 1, D_B), F32)],
        scratch_shapes=[pltpu.VMEM(((CONV_HALO + tm) * rv, LANES), F32),
                        pltpu.VMEM((tm * rv, LANES), F32),
                        pltpu.VMEM(((POOL_HALO + tm) * rv, LANES), F32),
                        pltpu.VMEM((tm * rv, LANES), F32)],
        compiler_params=_params(("arbitrary", "arbitrary")),
        name="seq_prompt",
    )(proj, proj, proj, proj, proj, p["ln_v_g"], p["ln_v_b"], p["w_spatial"], p["b_mix"],
      p["w_dwconv_rv"], p["b_dwconv_rv"], p["ln_conv_g"], p["ln_conv_b"])


def _seq_s_kernel(u_ref, v_ref, ga_ref, gb_ref, pin_ref, sconv_ref, spool_ref,
                  lnvg_ref, lnvb_ref, wv_ref, bv_ref, wconv_ref, bconv_ref, lncg_ref, lncb_ref,
                  acts_ref, vn_ref, xb_ref, *, t_len, nb, start):
    vn = []
    for t in range(t_len):
        vn_t = _layernorm(_gelu(v_ref[t].astype(F32)), lnvg_ref[...], lnvb_ref[...])
        vn_ref[t] = vn_t
        vn.append(vn_t)
        mixed = jnp.broadcast_to(bv_ref[t:t + 1, :], (nb, D_A))
        for s in range(t + 1):
            mixed = mixed + wv_ref[t * t_len + s:t * t_len + s + 1, :] * vn[s]
        acts_ref[t, :, 0:D_A] = (_gelu(u_ref[t].astype(F32)) * mixed).astype(BF16)

    hist = K_CONV - 1
    acc = [jnp.broadcast_to(bconv_ref[...], (nb, D_B)) for _ in range(t_len)]
    for j in range(hist + t_len):
        if j < hist:
            xj = sconv_ref[j]
        else:
            xj = ga_ref[j - hist].astype(F32) * jax.nn.sigmoid(gb_ref[j - hist].astype(F32))
            xb_ref[j - hist] = xj
        for t in range(t_len):
            k = j - t
            if 0 <= k < K_CONV:
                acc[t] = acc[t] + wconv_ref[k:k + 1, :] * xj
    for t in range(t_len):
        zb = _layernorm(acc[t], lncg_ref[...], lncb_ref[...])
        acts_ref[t, :, D_A:D_A + D_B] = (zb * jax.nn.sigmoid(zb)).astype(BF16)

    rows = [spool_ref[j] for j in range(POOL_BUF)]
    rows += [pin_ref[t].astype(F32) for t in range(t_len)]
    for t in range(t_len):
        for gi, w in enumerate(POOL_WINDOWS):
            lanes = slice(gi * D_CG, (gi + 1) * D_CG)
            s = rows[POOL_BUF + t][:, lanes]
            for j in range(1, w):
                s = s + rows[POOL_BUF + t - j][:, lanes]
            cnt = float(min(start + t + 1, w))
            acts_ref[t, :, D_A + D_B + gi * D_CG:D_A + D_B + (gi + 1) * D_CG] = (
                s / cnt - rows[POOL_BUF + t][:, lanes]).astype(BF16)


def _seq_s(grp, proj, sconv_t, spool_t, p, layer, nb=32):
    t_len, n_seq = grp.t_len, grp.n_seq
    proj3 = proj.reshape(t_len, n_seq, N_IN)
    hist = K_CONV - 1

    def col(c):
        return pl.BlockSpec((t_len, nb, 1024), lambda b: (0, b, c))

    def lay(*shape):
        return pl.BlockSpec((None,) + shape, lambda b: (layer,) + (0,) * len(shape))

    return pl.pallas_call(
        functools.partial(_seq_s_kernel, t_len=t_len, nb=nb, start=PAST_LEN),
        grid=(n_seq // nb,),
        in_specs=[col(COL_U), col(COL_V), col(COL_GA), col(COL_GB), col(COL_PIN),
                  pl.BlockSpec((None, hist, nb, D_B), lambda b: (layer, 0, b, 0)),
                  pl.BlockSpec((None, POOL_BUF, nb, D_C), lambda b: (layer, 0, b, 0)),
                  lay(1, D_A), lay(1, D_A), lay(t_len * t_len, D_A), lay(t_len, D_A),
                  lay(K_CONV, D_B), lay(1, D_B), lay(1, D_B), lay(1, D_B)],
        out_specs=[pl.BlockSpec((t_len, nb, 3 * 1024), lambda b: (0, b, 0)),
                   pl.BlockSpec((t_len, nb, D_A), lambda b: (0, b, 0)),
                   pl.BlockSpec((t_len, nb, D_B), lambda b: (0, b, 0))],
        out_shape=[jax.ShapeDtypeStruct((t_len, n_seq, 3 * 1024), BF16),
                   jax.ShapeDtypeStruct((t_len, n_seq, D_A), F32),
                   jax.ShapeDtypeStruct((t_len, n_seq, D_B), F32)],
        compiler_params=_params(("arbitrary",)),
        name="seq_sample",
    )(proj3, proj3, proj3, proj3, proj3, sconv_t, spool_t,
      p["ln_v_g"], p["ln_v_b"], p["w_spatial_vec"], p["b_spatial_vec"],
      p["w_dwconv"], p["b_dwconv"], p["ln_conv_g"], p["ln_conv_b"])


def _mixmm_kernel(acts_ref, g0, g1, g2, g3, g4, g5, bgate_ref, wa_ref, wb_ref, wp_ref, ps_ref,
                  o_ref, wa_s, wb_s, wp_s):
    @pl.when(pl.program_id(0) == 0)
    def _():
        wa_s[...] = wa_ref[...].astype(BF16)
        wb_s[...] = wb_ref[...].astype(BF16)
        wp_s[...] = wp_ref[...].astype(BF16)

    gate_refs = (g0, g1, g2, g3, g4, g5)

    def gate(branch, half):
        idx = 2 * branch + half
        pre = gate_refs[idx][...].astype(F32) + bgate_ref[:, idx * 1024:(idx + 1) * 1024]
        return jax.nn.sigmoid(pre)

    ya = jnp.dot(acts_ref[:, 0:D_A], wa_s[...], preferred_element_type=F32)
    yb = jnp.dot(acts_ref[:, D_A:D_A + D_B], wb_s[...], preferred_element_type=F32)
    for half in range(2):
        cols = slice(half * 1024, (half + 1) * 1024)
        yc = jnp.concatenate(
            [jnp.dot(acts_ref[:, D_A + D_B + g * D_CG:D_A + D_B + (g + 1) * D_CG], wp_s[g],
                     preferred_element_type=F32) for g in (2 * half, 2 * half + 1)],
            axis=-1) * ps_ref[:, cols]
        mix = gate(0, half) * ya[:, cols] + gate(1, half) * yb[:, cols] + gate(2, half) * yc
        o_ref[:, cols] = mix.astype(BF16)


def _mixmm(acts, proj, p, layer, tm):
    m = acts.shape[0]

    def gcol(c):
        return pl.BlockSpec((tm, 1024), lambda i: (i, COL_GATES + c))

    def lay(*shape):
        return _resident((None,) + shape, lambda i: (layer,) + (0,) * len(shape))

    return pl.pallas_call(
        _mixmm_kernel,
        grid=(m // tm,),
        in_specs=[pl.BlockSpec((tm, 3 * 1024), lambda i: (i, 0))] + [gcol(c) for c in range(6)] + [
            lay(1, 3 * D_MODEL), lay(D_A, D_MODEL), lay(D_B, D_MODEL),
            lay(N_POOL, D_CG, D_CG_OUT), lay(1, D_MODEL)],
        out_specs=pl.BlockSpec((tm, D_MODEL), lambda i: (i, 0)),
        out_shape=jax.ShapeDtypeStruct((m, D_MODEL), BF16),
        scratch_shapes=[pltpu.VMEM((D_A, D_MODEL), BF16), pltpu.VMEM((D_B, D_MODEL), BF16),
                        pltpu.VMEM((N_POOL, D_CG, D_CG_OUT), BF16)],
        compiler_params=_params(("arbitrary",)),
        name="mixmm",
    )(acts, proj, proj, proj, proj, proj, proj, p["b_gate"], p["w_a_out"], p["w_b_out"],
      p["w_pool_grp"], p["pool_scale"])


def _g1_kernel(a_ref, w_ref, x_ref, gate_ref, gpost_ref, gnext_ref, sc_ref, sh_ref,
               xo_ref, ho_ref, *scratch):
    tm = x_ref.shape[0]
    if scratch:
        wb, = scratch

        @pl.when(pl.program_id(0) == 0)
        def _():
            wb[...] = w_ref[...].astype(BF16)

        w = wb[...]
    else:
        w = w_ref[...]
    y = jnp.dot(a_ref[...], w, preferred_element_type=F32)
    x = x_ref[...] + _rows(gate_ref[...], tm) * _rmsnorm(y, gpost_ref[...])
    xo_ref[...] = x
    h = _rmsnorm(x, gnext_ref[...]) * (1.0 + _rows(sc_ref[...], tm)) + _rows(sh_ref[...], tm)
    ho_ref[...] = h.astype(BF16)


def _g1(grp, a, w, x, layer, c_gate, g_post, layer_next, c_scale, c_shift, g_next, tm, name):
    k = a.shape[1]
    scratch = [pltpu.VMEM((k, D_MODEL), BF16)] if w.dtype == F32 else []
    return pl.pallas_call(
        _g1_kernel,
        grid=(grp.rows // tm,),
        in_specs=[
            pl.BlockSpec((tm, k), lambda i: (i, 0)),
            _resident((None, k, D_MODEL), lambda i: (layer, 0, 0)),
            pl.BlockSpec((tm, D_MODEL), lambda i: (i, 0)),
            grp.mod_spec(tm, layer, c_gate),
            _vec_spec(D_MODEL, layer),
            _vec_spec(D_MODEL, layer_next),
            grp.mod_spec(tm, layer_next, c_scale),
            grp.mod_spec(tm, layer_next, c_shift),
        ],
        out_specs=[pl.BlockSpec((tm, D_MODEL), lambda i: (i, 0)),
                   pl.BlockSpec((tm, D_MODEL), lambda i: (i, 0))],
        out_shape=[jax.ShapeDtypeStruct((grp.rows, D_MODEL), F32),
                   jax.ShapeDtypeStruct((grp.rows, D_MODEL), BF16)],
        scratch_shapes=scratch,
        compiler_params=_params(("arbitrary",)),
        name=name,
    )(a, w, x, grp.mods, _vec(g_post), _vec(g_next), grp.mods, grp.mods)


def _ffn_up_kernel(h_ref, hs_ref, st_ref, wg_ref, wv0_ref, wv1_ref, cw_ref, cb_ref,
                   act_ref, new_ref, acts_ref, news_ref, wb, carry,
                   *, tm, tiles_per_seq, t_s, n_s):
    i = pl.program_id(1)
    is_prompt = i != SAMPLE_STEP
    seq_tile = _out_tile(i) % tiles_per_seq
    w0, w1, w2, b = cw_ref[0:1, :], cw_ref[1:2, :], cw_ref[2:3, :], cb_ref[...]

    @pl.when(i == 0)
    def _():
        wb[:, 0:FF_TILE] = wg_ref[...].astype(BF16)
        wb[:, FF_TILE:2 * FF_TILE - FF_SPLIT] = wv0_ref[:, FF_SPLIT:].astype(BF16)
        wb[:, 2 * FF_TILE - FF_SPLIT:] = wv1_ref[:, :FF_SPLIT].astype(BF16)

    @pl.when(jnp.logical_and(is_prompt, seq_tile == 0))
    def _():
        carry[...] = jnp.zeros(carry.shape, F32)

    @pl.when(is_prompt)
    def _():
        rc = ROW_CHUNK
        row = lax.broadcasted_iota(jnp.int32, (rc, 1), 0)
        h1, h2 = carry[SUBLANES - 1:SUBLANES, :], carry[SUBLANES - 2:SUBLANES - 1, :]
        gp = None
        for q in range(tm // rc):
            rows = slice(q * rc, (q + 1) * rc)
            acc = jnp.dot(h_ref[rows, :], wb[...], preferred_element_type=F32)
            gp, val = acc[:, :FF_TILE], acc[:, FF_TILE:]
            prev1 = jnp.where(row == 0, h1, pltpu.roll(gp, 1, axis=0))
            prev2 = jnp.where(row == 0, h2, jnp.where(row == 1, h1, pltpu.roll(gp, 2, axis=0)))
            gc = w0 * prev2 + w1 * prev1 + w2 * gp + b
            act_ref[rows, :] = (_gelu(gc) * val).astype(BF16)
            h1, h2 = gp[rc - 1:rc, :], gp[rc - 2:rc - 1, :]
        carry[...] = gp[rc - SUBLANES:rc, :]

        @pl.when(seq_tile == tiles_per_seq - 1)
        def _():
            new_ref[...] = gp[rc - (K_FFN - 1):rc, :]

    @pl.when(i == SAMPLE_STEP)
    def _():
        hist = K_FFN - 1
        acc = jnp.dot(hs_ref[...], wb[...], preferred_element_type=F32)
        gp, val = acc[:, :FF_TILE], acc[:, FF_TILE:]
        x = [st_ref[j] for j in range(hist)]
        x += [gp[t * n_s:(t + 1) * n_s, :] for t in range(t_s)]
        for t in range(t_s):
            gc = w0 * x[t] + w1 * x[t + 1] + w2 * x[t + 2] + b
            acts_ref[t * n_s:(t + 1) * n_s, :] = (
                _gelu(gc) * val[t * n_s:(t + 1) * n_s, :]).astype(BF16)
        for j in range(hist):
            news_ref[j] = x[t_s + j]


def _ffn_up(grp, grp_s, h, h_s, sffn_t, w_up, p, layer, tm):
    tiles_per_seq = grp.t_len // tm
    n_i = grp.rows // tm
    n_j = pl.cdiv(D_FF, FF_TILE)
    val0 = D_FF // FF_TILE
    hist = K_FFN - 1
    t_s, n_s = grp_s.t_len, grp_s.n_seq
    assert n_i > SAMPLE_STEP

    def wblk(off):
        return pl.BlockSpec((None, D_MODEL, FF_TILE), lambda j, i: (layer, 0, j + off))

    return pl.pallas_call(
        functools.partial(_ffn_up_kernel, tm=tm, tiles_per_seq=tiles_per_seq, t_s=t_s, n_s=n_s),
        grid=(n_j, n_i + 1),
        in_specs=[pl.BlockSpec((tm, D_MODEL), lambda j, i: (_in_tile(i), 0)),
                  _resident((grp_s.rows, D_MODEL), lambda j, i: (0, 0)),
                  pl.BlockSpec((None, hist, n_s, FF_TILE), lambda j, i: (layer, 0, 0, j)),
                  wblk(0), wblk(val0), wblk(val0 + 1),
                  pl.BlockSpec((None, K_FFN, FF_TILE), lambda j, i: (layer, 0, j)),
                  pl.BlockSpec((None, 1, FF_TILE), lambda j, i: (layer, 0, j))],
        out_specs=[pl.BlockSpec((tm, FF_TILE), lambda j, i: (_out_tile(i), j)),
                   pl.BlockSpec((None, hist, FF_TILE),
                                lambda j, i: (_out_tile(i) // tiles_per_seq, 0, j)),
                   pl.BlockSpec((grp_s.rows, FF_TILE), lambda j, i: (0, j)),
                   pl.BlockSpec((hist, n_s, FF_TILE), lambda j, i: (0, 0, j))],
        out_shape=[jax.ShapeDtypeStruct((grp.rows, D_FF), BF16),
                   jax.ShapeDtypeStruct((grp.n_seq, hist, D_FF), F32),
                   jax.ShapeDtypeStruct((grp_s.rows, D_FF), BF16),
                   jax.ShapeDtypeStruct((hist, n_s, D_FF), F32)],
        scratch_shapes=[pltpu.VMEM((D_MODEL, 2 * FF_TILE), BF16),
                        pltpu.VMEM((SUBLANES, FF_TILE), F32)],
        compiler_params=_params(("arbitrary", "arbitrary")),
        name="ffn_up",
    )(h, h_s, sffn_t, w_up, w_up, w_up, p["w_ffn_conv"], p["b_ffn_conv"])


SHIFT_M, SCALE_M, GATE_M, SHIFT_F, SCALE_F, GATE_F = range(6)


def _small_params(p, t_s):
    ws, bs = p["w_spatial"], p["b_spatial"]
    b_mix = jnp.repeat(bs.transpose(0, 2, 1), CHUNK, axis=2)
    w_vec = jnp.repeat(ws[:, :, :t_s, :t_s].transpose(0, 2, 3, 1), CHUNK, axis=3)
    return dict(
        pool_scale=_vec(p["pool_scale"]), b_gate=_vec(p["b_gate"]),
        ln_v_g=_vec(p["ln_v_g"]), ln_v_b=_vec(p["ln_v_b"]),
        w_spatial=ws, b_mix=b_mix,
        w_spatial_vec=w_vec.reshape(DEPTH, t_s * t_s, D_A), b_spatial_vec=b_mix[:, :t_s],
        w_dwconv=p["w_dwconv"], b_dwconv=_vec(p["b_dwconv"]),
        w_dwconv_rv=p["w_dwconv"].reshape(DEPTH, K_CONV * ROW_VREGS, LANES),
        b_dwconv_rv=p["b_dwconv"].reshape(DEPTH, ROW_VREGS, LANES),
        ln_conv_g=_vec(p["ln_conv_g"]), ln_conv_b=_vec(p["ln_conv_b"]),
        w_ffn_conv=p["w_ffn_conv"], b_ffn_conv=_vec(p["b_ffn_conv"]),
        w_a_out=p["w_a_out"], w_b_out=p["w_b_out"], w_pool_grp=p["w_pool_grp"],
    )


def kernel(x_prompt, x_sample, c_prompt, c_sample, state_conv, state_pool, state_ffn_conv, ada_w, ada_b, g_pre_mix, g_post_mix, g_pre_ffn, g_post_ffn, w_in, b_gate, ln_v_g, ln_v_b, w_spatial, b_spatial, w_a_out, w_dwconv, b_dwconv, ln_conv_g, ln_conv_b, w_b_out, w_pool_grp, pool_scale, w_o, w_up, w_ffn_conv, b_ffn_conv, w_down):
    n_p, t_p, _ = x_prompt.shape
    n_s, t_s, _ = x_sample.shape
    sp = _small_params(dict(
        b_gate=b_gate, ln_v_g=ln_v_g, ln_v_b=ln_v_b, w_spatial=w_spatial, b_spatial=b_spatial,
        w_a_out=w_a_out, w_dwconv=w_dwconv, b_dwconv=b_dwconv, ln_conv_g=ln_conv_g,
        ln_conv_b=ln_conv_b, w_b_out=w_b_out, w_pool_grp=w_pool_grp, pool_scale=pool_scale,
        w_ffn_conv=w_ffn_conv, b_ffn_conv=b_ffn_conv), t_s)
    w_down_bf = w_down.astype(BF16)

    pad = (-(n_p + n_s)) % SUBLANES
    c_all = jnp.concatenate([c_sample, c_prompt, jnp.zeros((pad, D_MODEL), F32)], axis=0)
    mods = _mods(c_all, ada_w, ada_b)
    gs_ = _Group(n_s, t_s, True, mods)
    gp_ = _Group(n_p, t_p, False, mods[:, n_s:n_s + n_p].reshape(DEPTH, n_p, 1, 6 * D_MODEL))

    sconv_t = state_conv.transpose(0, 2, 1, 3)
    spool_t = state_pool.transpose(0, 2, 1, 3)
    sffn_t = state_ffn_conv.transpose(0, 2, 1, 3)

    x_p = x_prompt.reshape(n_p * t_p, D_MODEL)
    x_s = x_sample.transpose(1, 0, 2).reshape(t_s * n_s, D_MODEL)
    tm_mm, tm_s = t_p, t_s * n_s
    tm_seq = 256

    h_p = _prenorm(gp_, x_p, g_pre_mix, 0, SCALE_M, SHIFT_M, 512)
    h_s = _prenorm(gs_, x_s, g_pre_mix, 0, SCALE_M, SHIFT_M, tm_s)

    outs = {k: [] for k in ("conv_p", "xb_s", "pool_p", "pin_s", "ffn_p", "gp_s", "v_s")}
    pin_lo = COL_PIN * 1024
    for l in range(DEPTH):
        nxt = min(l + 1, DEPTH - 1)

        proj_p, proj_s = _mm(h_p, h_s, w_in, l, tm_mm, 1024, "proj")
        acts_p, nconv_p = _seq_p(gp_, proj_p, sp, l, tm_seq)
        acts_s, vn_s, xb_s = _seq_s(gs_, proj_s, sconv_t, spool_t, sp, l)
        mix_p = _mixmm(acts_p, proj_p, sp, l, tm_seq)
        mix_s = _mixmm(acts_s.reshape(gs_.rows, 3 * 1024), proj_s, sp, l, tm_seq)
        x_p, h_p = _g1(gp_, mix_p, w_o, x_p, l, GATE_M, g_post_mix, l, SCALE_F, SHIFT_F,
                       g_pre_ffn, tm_seq, "wo_prompt")
        x_s, h_s = _g1(gs_, mix_s, w_o, x_s, l, GATE_M, g_post_mix, l, SCALE_F, SHIFT_F,
                       g_pre_ffn, tm_s, "wo_sample")

        act_p, nffn_p, act_s, nffn_s = _ffn_up(gp_, gs_, h_p, h_s, sffn_t, w_up, sp, l, tm_mm)
        x_p, h_p = _g1(gp_, act_p, w_down_bf, x_p, l, GATE_F, g_post_ffn, nxt, SCALE_M, SHIFT_M,
                       g_pre_mix, tm_seq, "down_prompt")
        x_s, h_s = _g1(gs_, act_s, w_down_bf, x_s, l, GATE_F, g_post_ffn, nxt, SCALE_M, SHIFT_M,
                       g_pre_mix, tm_s, "down_sample")

        outs["conv_p"].append(nconv_p)
        outs["xb_s"].append(xb_s)
        outs["pool_p"].append(
            proj_p.reshape(n_p, t_p, N_IN)[:, t_p - POOL_BUF:, pin_lo:pin_lo + D_C])
        outs["pin_s"].append(proj_s.reshape(t_s, n_s, N_IN)[:, :, pin_lo:pin_lo + D_C])
        outs["ffn_p"].append(nffn_p)
        outs["gp_s"].append(nffn_s)
        outs["v_s"].append(vn_s)

    def seq_major(parts):
        return jnp.stack(parts).astype(F32).transpose(0, 2, 1, 3)

    y_p = x_p.reshape(n_p, t_p, D_MODEL)
    y_s = x_s.reshape(t_s, n_s, D_MODEL).transpose(1, 0, 2)
    new_conv_s = jnp.concatenate([state_conv[:, :, t_s:], seq_major(outs["xb_s"])], axis=2)
    new_pool_s = jnp.concatenate([state_pool[:, :, t_s:], seq_major(outs["pin_s"])], axis=2)
    return (y_p, y_s, jnp.stack(outs["conv_p"]), new_conv_s,
            jnp.stack(outs["pool_p"]).astype(F32), new_pool_s,
            jnp.stack(outs["ffn_p"]).astype(F32), seq_major(outs["gp_s"]),
            seq_major(outs["v_s"]))
```
